```python
import jax, jax.numpy as jnp
from jax import lax
import numpy as np

D_MODEL = 1024
BATCH = 16
SEQ = 256
DEPTH = 4
DEC_BATCH = 8
DEC_SEQ = 1024
PAST_LEN = 512

GRID_W = 64
HEAD_DIM = 64
NA_HEADS = 8
NA_WIN_R = 8
NA_WIN_C = 16
CONV_C = 512
CONV_W = 31
GQA_HEADS = 8
GQA_KV_HEADS = 2
GQA_GROUP = GQA_HEADS // GQA_KV_HEADS
LRU_W = 512
LRU_BLOCKS = 8
LRU_CONV_W = 4
LRU_C = 8.0
N_BRANCH = 4
BRANCH_W = 512
D_FF = 2816
Q_BLOCK = 128
ROPE_THETA = 10000.0
EPS = 1e-6
NEG_INF = -1e30
HALF = 0.5

NA_W = NA_HEADS * HEAD_DIM
GQA_QW = GQA_HEADS * HEAD_DIM
GQA_KVW = GQA_KV_HEADS * HEAD_DIM
OFF_NA = 0
OFF_CONV = OFF_NA + 3 * NA_W
OFF_GQA = OFF_CONV + 2 * CONV_C
OFF_LRU = OFF_GQA + GQA_QW + 2 * GQA_KVW
OFF_GATE = OFF_LRU + LRU_W
IN_COLS = OFF_GATE + N_BRANCH * D_MODEL

kernel_name = 'hybrid_na_conformer_gqa_rglru_prefix_dit'


def rmsnorm(x, g):
    xf = x.astype(jnp.float32)
    y = xf * lax.rsqrt(jnp.mean(xf * xf, axis=-1, keepdims=True) + EPS)
    return (y * g.astype(jnp.float32)).astype(x.dtype)


def layernorm(x, g, b):
    xf = x.astype(jnp.float32)
    mu = jnp.mean(xf, axis=-1, keepdims=True)
    var = jnp.mean(jnp.square(xf - mu), axis=-1, keepdims=True)
    y = (xf - mu) * lax.rsqrt(var + EPS)
    return (y * g.astype(jnp.float32) + b.astype(jnp.float32)).astype(x.dtype)


def adaln(cvec, w, b):
    m = jax.nn.silu(cvec) @ w + b
    return m.reshape(cvec.shape[0], 3, 3, D_MODEL)


def modulate(x, g_pre, mod_s):
    return rmsnorm(x, g_pre) * (1.0 + mod_s[:, None, 1]) + mod_s[:, None, 0]


def add_residual(x, y, g_post, mod_s, res_w):
    return x + res_w * mod_s[:, None, 2] * rmsnorm(y, g_post)


def swiglu(h, w1, w2):
    a, u = jnp.split(h @ w1, 2, axis=-1)
    return (jax.nn.silu(a) * u) @ w2


def depthwise_conv(x, w, pad):
    return lax.conv_general_dilated(x, w[:, None, :].astype(x.dtype), window_strides=(1,), padding=[pad],
                                    dimension_numbers=('NWC', 'WIO', 'NWC'), feature_group_count=x.shape[-1])


def rope_2d(n):
    t = jnp.arange(n)
    row = (t // GRID_W).astype(jnp.float32)
    col = (t % GRID_W).astype(jnp.float32)
    half = HEAD_DIM // 2
    freqs = ROPE_THETA ** (-jnp.arange(0, half, 2, dtype=jnp.float32) / half)
    ar = row[:, None] * freqs
    ac = col[:, None] * freqs
    return jnp.cos(ar), jnp.sin(ar), jnp.cos(ac), jnp.sin(ac)


def apply_rope_2d(x, tabs):
    cr, sr, cc, sc = tabs
    xf = x.astype(jnp.float32)

    def rot(xh, c, s):
        x1, x2 = jnp.split(xh, 2, axis=-1)
        c = c[None, :, None, :]
        s = s[None, :, None, :]
        return jnp.concatenate([x1 * c - x2 * s, x1 * s + x2 * c], axis=-1)

    xr, xc = jnp.split(xf, 2, axis=-1)
    return jnp.concatenate([rot(xr, cr, sr), rot(xc, cc, sc)], axis=-1).astype(x.dtype)


def block_attention(q, k, v):
    B, N, Hk, G, Dh = q.shape
    qb = jnp.moveaxis(q.reshape(B, N // Q_BLOCK, Q_BLOCK, Hk, G, Dh), 1, 0)
    scale = Dh ** -0.5

    def one(qi):
        s = jnp.einsum('bqhgd,bkhd->bhgqk', qi, k, preferred_element_type=jnp.float32) * scale
        p = jax.nn.softmax(s, axis=-1)
        return jnp.einsum('bhgqk,bkhd->bqhgd', p.astype(v.dtype), v)

    o = lax.map(one, qb)
    return jnp.moveaxis(o, 0, 1).reshape(B, N, Hk * G * Dh)


def na_latent(q, k, v, ck, cv, rpb):
    B, N, H, Dh = q.shape
    rows = N // GRID_W
    wr = min(NA_WIN_R, rows)
    r = jnp.arange(rows)
    r0 = jnp.clip(r - wr // 2, 0, rows - wr)
    row_idx = r0[:, None] + jnp.arange(wr)[None, :]
    cidx = jnp.arange(GRID_W)
    c0 = jnp.clip(cidx - NA_WIN_C // 2, 0, GRID_W - NA_WIN_C)
    col_ok = (cidx[None, :] >= c0[:, None]) & (cidx[None, :] < c0[:, None] + NA_WIN_C)
    qg = q.reshape(B, rows, GRID_W, H, Dh)
    kg = k.reshape(B, rows, GRID_W, H, Dh)[:, row_idx]
    vg = v.reshape(B, rows, GRID_W, H, Dh)[:, row_idx]
    scale = Dh ** -0.5
    s_lat = jnp.einsum('brqhd,brjkhd->bhrqjk', qg, kg, preferred_element_type=jnp.float32) * scale
    dr = row_idx - r[:, None]
    dc = jnp.clip(cidx[None, :] - cidx[:, None], -(NA_WIN_C - 1), NA_WIN_C - 1)
    bias = rpb[:, (dr + NA_WIN_R - 1)[:, None, :, None], (dc + NA_WIN_C - 1)[None, :, None, :]]
    s_lat = jnp.where(col_ok[:, None, :], s_lat + bias[None].astype(jnp.float32), NEG_INF)
    s_lat = s_lat.reshape(B, H, rows, GRID_W, wr * GRID_W)
    s_ctx = jnp.einsum('brqhd,bkhd->bhrqk', qg, ck, preferred_element_type=jnp.float32) * scale
    p = jax.nn.softmax(jnp.concatenate([s_lat, s_ctx], axis=-1), axis=-1).astype(v.dtype)
    p_lat, p_ctx = p[..., :wr * GRID_W], p[..., wr * GRID_W:]
    o = jnp.einsum('bhrqk,brkhd->brqhd', p_lat, vg.reshape(B, rows, wr * GRID_W, H, Dh))
    o = o + jnp.einsum('bhrqk,bkhd->brqhd', p_ctx, cv)
    return o.reshape(B, N, H * Dh)


def conv_branch(glu, dw, ln_g, ln_b):
    a, g = jnp.split(glu, 2, axis=-1)
    u = depthwise_conv(a * jax.nn.sigmoid(g), dw, (CONV_W // 2, CONV_W // 2))
    return jax.nn.silu(layernorm(u, ln_g, ln_b))


def lru_coeffs(xc, wr, br, wi, bi, lam):
    B, L, W = xc.shape
    xb = xc.reshape(B, L, LRU_BLOCKS, W // LRU_BLOCKS)
    r = jax.nn.sigmoid((jnp.einsum('blni,nij->blnj', xb, wr).reshape(B, L, W) + br).astype(jnp.float32))
    i = jax.nn.sigmoid((jnp.einsum('blni,nij->blnj', xb, wi).reshape(B, L, W) + bi).astype(jnp.float32))
    log_a = LRU_C * r * jax.nn.log_sigmoid(lam.astype(jnp.float32))
    a = jnp.exp(log_a)
    b = jnp.sqrt(-jnp.expm1(2.0 * log_a)) * (i * xc.astype(jnp.float32))
    return a, b


def linear_scan(a, b, h0):
    def comb(e1, e2):
        a1, b1 = e1
        a2, b2 = e2
        return a1 * a2, a2 * b1 + b2

    A, Bc = lax.associative_scan(comb, (a, b), axis=1)
    return A * h0[:, None, :] + Bc


def lru_branch(x, lp, h0):
    xc = depthwise_conv(x, lp['lru_conv_w'], (LRU_CONV_W // 2, LRU_CONV_W - 1 - LRU_CONV_W // 2)) + lp['lru_conv_b']
    a_f, b_f = lru_coeffs(xc, lp['lru_wr'][0], lp['lru_br'][0], lp['lru_wi'][0], lp['lru_bi'][0], lp['lru_lambda'][0])
    h_f = linear_scan(a_f, b_f, h0[:, 0])
    a_b, b_b = lru_coeffs(xc, lp['lru_wr'][1], lp['lru_br'][1], lp['lru_wi'][1], lp['lru_bi'][1], lp['lru_lambda'][1])
    h_b = jnp.flip(linear_scan(jnp.flip(a_b, 1), jnp.flip(b_b, 1), h0[:, 1]), 1)
    y = (h_f + h_b).astype(x.dtype)
    final = jnp.stack([h_f[:, -1], h_b[:, 0]], axis=1)
    return y, final


def split_proj(p):
    B, N, _ = p.shape

    def heads(lo, n_heads):
        return p[..., lo:lo + n_heads * HEAD_DIM].reshape(B, N, n_heads, HEAD_DIM)

    na_q = heads(OFF_NA, NA_HEADS)
    na_k = heads(OFF_NA + NA_W, NA_HEADS)
    na_v = heads(OFF_NA + 2 * NA_W, NA_HEADS)
    glu = p[..., OFF_CONV:OFF_GQA]
    gq_q = heads(OFF_GQA, GQA_HEADS)
    gq_k = heads(OFF_GQA + GQA_QW, GQA_KV_HEADS)
    gq_v = heads(OFF_GQA + GQA_QW + GQA_KVW, GQA_KV_HEADS)
    lru_x = p[..., OFF_LRU:OFF_GATE]
    gate_logits = p[..., OFF_GATE:]
    return na_q, na_k, na_v, glu, gq_q, gq_k, gq_v, lru_x, gate_logits


def merge_branches(branches, gate_logits, w_branch, w_out):
    B, N, _ = gate_logits.shape
    stacked = jnp.stack(branches, axis=2)
    proj = jnp.einsum('bnkc,kcd->bnkd', stacked, w_branch)
    g = jax.nn.sigmoid(gate_logits).reshape(B, N, N_BRANCH, D_MODEL)
    return jnp.sum(g * proj, axis=2) @ w_out


def mixer_context(h, lp):
    B, N, _ = h.shape
    na_q, na_k, na_v, glu, gq_q, gq_k, gq_v, lru_x, gl = split_proj(h @ lp['w_in'])
    o_na = block_attention(na_q[:, :, :, None, :], na_k, na_v)
    o_conv = conv_branch(glu, lp['conv_dw'], lp['conv_ln_g'], lp['conv_ln_b'])
    gq_q = rmsnorm(gq_q, lp['q_norm'])
    gq_k = rmsnorm(gq_k, lp['k_norm'])
    o_gqa = block_attention(gq_q.reshape(B, N, GQA_KV_HEADS, GQA_GROUP, HEAD_DIM), gq_k, gq_v)
    o_lru, lru_state = lru_branch(lru_x, lp, jnp.zeros((B, 2, LRU_W), jnp.float32))
    y = merge_branches((o_na, o_conv, o_gqa, o_lru), gl, lp['w_branch'], lp['w_out'])
    return y, na_k, na_v, gq_k, gq_v, lru_state


def mixer_latent(h, lp, tabs, ck_na, cv_na, ck_gq, cv_gq, h0):
    B, N, _ = h.shape
    na_q, na_k, na_v, glu, gq_q, gq_k, gq_v, lru_x, gl = split_proj(h @ lp['w_in'])
    o_na = na_latent(na_q, na_k, na_v, ck_na.astype(h.dtype), cv_na.astype(h.dtype), lp['na_rpb'])
    o_conv = conv_branch(glu, lp['conv_dw'], lp['conv_ln_g'], lp['conv_ln_b'])
    gq_q = apply_rope_2d(rmsnorm(gq_q, lp['q_norm']), tabs)
    gq_k = apply_rope_2d(rmsnorm(gq_k, lp['k_norm']), tabs)
    keys = jnp.concatenate([gq_k, ck_gq.astype(gq_k.dtype)], axis=1)
    vals = jnp.concatenate([gq_v, cv_gq.astype(gq_v.dtype)], axis=1)
    o_gqa = block_attention(gq_q.reshape(B, N, GQA_KV_HEADS, GQA_GROUP, HEAD_DIM), keys, vals)
    o_lru, _ = lru_branch(lru_x, lp, h0.astype(jnp.float32))
    return merge_branches((o_na, o_conv, o_gqa, o_lru), gl, lp['w_branch'], lp['w_out'])


def setup_inputs(seed: int = 0) -> dict:
    key = jax.random.key(seed)
    ks = jax.random.split(key, 32)
    f32 = jnp.float32

    def nrm(k, shape, scale):
        return jax.random.normal(k, shape, f32) * scale

    bw = LRU_W // LRU_BLOCKS
    u = jax.random.uniform(ks[27], (DEPTH, 2, LRU_W), f32, 0.9, 0.999)
    a0 = u ** (1.0 / LRU_C)
    lam = jnp.log(a0) - jnp.log1p(-a0)
    return {
        'x_prompt': nrm(ks[0], (BATCH, SEQ, D_MODEL), 1.0),
        'x_sample': nrm(ks[1], (DEC_BATCH, DEC_SEQ, D_MODEL), 1.0),
        'c': nrm(ks[2], (DEC_BATCH, D_MODEL), 1.0),
        'cache_na_k': nrm(ks[3], (DEC_BATCH, DEPTH, PAST_LEN, NA_HEADS, HEAD_DIM), 1.0),
        'cache_na_v': nrm(ks[4], (DEC_BATCH, DEPTH, PAST_LEN, NA_HEADS, HEAD_DIM), 1.0),
        'cache_gqa_k': nrm(ks[5], (DEC_BATCH, DEPTH, PAST_LEN, GQA_KV_HEADS, HEAD_DIM), 1.0),
        'cache_gqa_v': nrm(ks[6], (DEC_BATCH, DEPTH, PAST_LEN, GQA_KV_HEADS, HEAD_DIM), 1.0),
        'state_lru': nrm(ks[7], (DEC_BATCH, DEPTH, 2, LRU_W), 0.5),
        'c_ctx': nrm(ks[8], (D_MODEL,), 1.0),
        'w_ada': nrm(ks[9], (DEPTH, D_MODEL, 9 * D_MODEL), 0.3 * D_MODEL ** -0.5),
        'b_ada': nrm(ks[10], (DEPTH, 9 * D_MODEL), 0.02),
        'norm_g': 1.0 + nrm(ks[11], (DEPTH, 6, D_MODEL), 0.02),
        'ffn_w1': nrm(ks[12], (DEPTH, 2, D_MODEL, 2 * D_FF), D_MODEL ** -0.5),
        'ffn_w2': nrm(ks[13], (DEPTH, 2, D_FF, D_MODEL), D_FF ** -0.5),
        'w_in': nrm(ks[14], (DEPTH, D_MODEL, IN_COLS), D_MODEL ** -0.5),
        'na_rpb': nrm(ks[15], (DEPTH, NA_HEADS, 2 * NA_WIN_R - 1, 2 * NA_WIN_C - 1), 0.1),
        'conv_dw': nrm(ks[16], (DEPTH, CONV_W, CONV_C), CONV_W ** -0.5),
        'conv_ln_g': 1.0 + nrm(ks[17], (DEPTH, CONV_C), 0.02),
        'conv_ln_b': nrm(ks[18], (DEPTH, CONV_C), 0.02),
        'gqa_q_norm': 1.0 + nrm(ks[19], (DEPTH, HEAD_DIM), 0.02),
        'gqa_k_norm': 1.0 + nrm(ks[20], (DEPTH, HEAD_DIM), 0.02),
        'lru_conv_w': nrm(ks[21], (DEPTH, LRU_CONV_W, LRU_W), LRU_CONV_W ** -0.5),
        'lru_conv_b': nrm(ks[22], (DEPTH, LRU_W), 0.02),
        'lru_wr': nrm(ks[23], (DEPTH, 2, LRU_BLOCKS, bw, bw), bw ** -0.5),
        'lru_br': nrm(ks[24], (DEPTH, 2, LRU_W), 0.02),
        'lru_wi': nrm(ks[25], (DEPTH, 2, LRU_BLOCKS, bw, bw), bw ** -0.5),
        'lru_bi': nrm(ks[26], (DEPTH, 2, LRU_W), 0.02),
        'lru_lambda': lam,
        'w_branch': nrm(ks[28], (DEPTH, N_BRANCH, BRANCH_W, D_MODEL), BRANCH_W ** -0.5),
        'w_out': nrm(ks[29], (DEPTH, D_MODEL, D_MODEL), D_MODEL ** -0.5),
    }


def reference(x_prompt, x_sample, c, cache_na_k, cache_na_v, cache_gqa_k, cache_gqa_v, state_lru, c_ctx,
              w_ada, b_ada, norm_g, ffn_w1, ffn_w2, w_in, na_rpb, conv_dw, conv_ln_g, conv_ln_b,
              gqa_q_norm, gqa_k_norm, lru_conv_w, lru_conv_b, lru_wr, lru_br, lru_wi, lru_bi, lru_lambda,
              w_branch, w_out):
    tabs = rope_2d(x_sample.shape[1])
    xp, xs = x_prompt, x_sample
    nk_l, nv_l, gk_l, gv_l, st_l = [], [], [], [], []
    for l in range(DEPTH):
        lp = {'w_in': w_in[l], 'na_rpb': na_rpb[l], 'conv_dw': conv_dw[l], 'conv_ln_g': conv_ln_g[l],
              'conv_ln_b': conv_ln_b[l], 'q_norm': gqa_q_norm[l], 'k_norm': gqa_k_norm[l],
              'lru_conv_w': lru_conv_w[l], 'lru_conv_b': lru_conv_b[l], 'lru_wr': lru_wr[l], 'lru_br': lru_br[l],
              'lru_wi': lru_wi[l], 'lru_bi': lru_bi[l], 'lru_lambda': lru_lambda[l],
              'w_branch': w_branch[l], 'w_out': w_out[l]}
        g = norm_g[l]
        mod_p = adaln(c_ctx[None, :], w_ada[l], b_ada[l])
        mod_s = adaln(c, w_ada[l], b_ada[l])
        xp = add_residual(xp, swiglu(modulate(xp, g[0], mod_p[:, 0]), ffn_w1[l, 0], ffn_w2[l, 0]), g[1], mod_p[:, 0], HALF)
        y, nk, nv, gk, gv, st = mixer_context(modulate(xp, g[2], mod_p[:, 1]), lp)
        xp = add_residual(xp, y, g[3], mod_p[:, 1], 1.0)
        xp = add_residual(xp, swiglu(modulate(xp, g[4], mod_p[:, 2]), ffn_w1[l, 1], ffn_w2[l, 1]), g[5], mod_p[:, 2], HALF)
        nk_l.append(nk)
        nv_l.append(nv)
        gk_l.append(gk)
        gv_l.append(gv)
        st_l.append(st)
        xs = add_residual(xs, swiglu(modulate(xs, g[0], mod_s[:, 0]), ffn_w1[l, 0], ffn_w2[l, 0]), g[1], mod_s[:, 0], HALF)
        y = mixer_latent(modulate(xs, g[2], mod_s[:, 1]), lp, tabs, cache_na_k[:, l], cache_na_v[:, l],
                         cache_gqa_k[:, l], cache_gqa_v[:, l], state_lru[:, l])
        xs = add_residual(xs, y, g[3], mod_s[:, 1], 1.0)
        xs = add_residual(xs, swiglu(modulate(xs, g[4], mod_s[:, 2]), ffn_w1[l, 1], ffn_w2[l, 1]), g[5], mod_s[:, 2], HALF)
    new_na_k = jnp.stack(nk_l, axis=1)
    new_na_v = jnp.stack(nv_l, axis=1)
    new_gqa_k = jnp.stack(gk_l, axis=1)
    new_gqa_v = jnp.stack(gv_l, axis=1)
    new_lru_state = jnp.stack(st_l, axis=1)
    return (xp, xs, new_na_k, new_na_v, new_gqa_k, new_gqa_v, new_lru_state)
```

```python
import functools

import numpy as np
import jax
import jax.numpy as jnp
from jax import lax
from jax.experimental import pallas as pl
from jax.experimental.pallas import tpu as pltpu

D_MODEL = 1024
BATCH = 16
SEQ = 256
DEPTH = 4
DEC_BATCH = 8
DEC_SEQ = 1024
PAST_LEN = 512
GRID_W = 64
GRID_ROWS = DEC_SEQ // GRID_W
HEAD_DIM = 64
NA_HEADS = 8
NA_WIN_R = 8
NA_WIN_C = 16
CONV_C = 512
CONV_W = 31
GQA_HEADS = 8
GQA_KV_HEADS = 2
LRU_W = 512
LRU_BLOCKS = 8
LRU_CONV_W = 4
LRU_C = 8.0
N_BRANCH = 4
BRANCH_W = 512
D_FF = 2816
ROPE_THETA = 10000.0
EPS = 1e-6
NEG_INF = -1e30

NA_W = NA_HEADS * HEAD_DIM
GQA_QW = GQA_HEADS * HEAD_DIM
GQA_KVW = GQA_KV_HEADS * HEAD_DIM
OFF_NA = 0
OFF_CONV = OFF_NA + 3 * NA_W
OFF_GQA = OFF_CONV + 2 * CONV_C
OFF_LRU = OFF_GQA + GQA_QW + 2 * GQA_KVW
OFF_GATE = OFF_LRU + LRU_W
IN_COLS = OFF_GATE + N_BRANCH * D_MODEL
N_MOD = 9 * D_MODEL

F32 = jnp.float32
BF16 = jnp.bfloat16

LANES = 128
HEAD_PAIRS = NA_HEADS // 2
ATTN_SCALE = HEAD_DIM ** -0.5
ADA_ROWS = 16
VMEM_LIMIT = 56 * 1024 * 1024

TM_FFN = 512
TM_PROJ = 256
TM_MERGE = 256
FF_CHUNK = 256
ADA_TN = 1152
CONV_ROWS = 64
CONV_PAD = 16
LRU_PAD = 8
LRU_GATE_ROWS = 256
LRU_HALF = LRU_W // 2
NA_QROWS = 2
NA_WROWS = 10
GQA_QBLK = 256


def _cparams(sem):
    return pltpu.CompilerParams(dimension_semantics=sem, vmem_limit_bytes=VMEM_LIMIT)


def _rms(x):
    return x * lax.rsqrt(jnp.mean(x * x, axis=-1, keepdims=True) + EPS)


def _modulate(x, g_pre, mod, sub):
    shift = mod[3 * sub:3 * sub + 1]
    scale = mod[3 * sub + 1:3 * sub + 2]
    return (_rms(x) * g_pre) * (1.0 + scale) + shift


def _resident(shape):
    nd = len(shape)
    return pl.BlockSpec(shape, lambda *_: (0,) * nd)


def _adaln_kernel(c_ref, w_ref, b_ref, o_ref):
    c = c_ref[...]
    s = (c * jax.nn.sigmoid(c)).astype(BF16)
    o_ref[0] = jnp.dot(s, w_ref[0].astype(BF16), preferred_element_type=F32) + b_ref[0]


def _adaln(c_all, w_ada, b_ada):
    return pl.pallas_call(
        _adaln_kernel,
        grid=(DEPTH, N_MOD // ADA_TN),
        in_specs=[
            pl.BlockSpec((ADA_ROWS, D_MODEL), lambda l, n: (0, 0)),
            pl.BlockSpec((1, D_MODEL, ADA_TN), lambda l, n: (l, 0, n)),
            pl.BlockSpec((1, 1, ADA_TN), lambda l, n: (l, 0, n)),
        ],
        out_specs=pl.BlockSpec((1, ADA_ROWS, ADA_TN), lambda l, n: (l, 0, n)),
        out_shape=jax.ShapeDtypeStruct((DEPTH, ADA_ROWS, N_MOD), F32),
        compiler_params=_cparams(("arbitrary", "arbitrary")),
        name="adaln",
    )(c_all, w_ada, b_ada.reshape(DEPTH, 1, N_MOD))


def _ffn_kernel(x_ref, mod_ref, g_ref, w1_ref, w2_ref, o_ref, acc_ref, *, sub, res_w):
    x = x_ref[...]
    mod = mod_ref[0]
    h = _modulate(x, g_ref[2 * sub:2 * sub + 1], mod, sub).astype(BF16)
    for c in range(D_FF // FF_CHUNK):
        lo = c * FF_CHUNK
        a = jnp.dot(h, w1_ref[:, lo:lo + FF_CHUNK], preferred_element_type=F32)
        u = jnp.dot(h, w1_ref[:, D_FF + lo:D_FF + lo + FF_CHUNK], preferred_element_type=F32)
        act = (a * jax.nn.sigmoid(a) * u).astype(BF16)
        part = jnp.dot(act, w2_ref[lo:lo + FF_CHUNK, :], preferred_element_type=F32)
        if c == 0:
            acc_ref[...] = part
        else:
            acc_ref[...] += part
    gate = mod[3 * sub + 2:3 * sub + 3]
    o_ref[...] = x + (res_w * gate) * (_rms(acc_ref[...]) * g_ref[2 * sub + 1:2 * sub + 2])


def _ffn(x, mod, g, w1, w2, *, sub, tokens_per_mod):
    T = x.shape[0]
    tm = TM_FFN
    return pl.pallas_call(
        functools.partial(_ffn_kernel, sub=sub, res_w=0.5),
        grid=(T // tm,),
        in_specs=[
            pl.BlockSpec((tm, D_MODEL), lambda i: (i, 0)),
            pl.BlockSpec((1, 9, D_MODEL), lambda i: (i * tm // tokens_per_mod, 0, 0)),
            _resident((6, D_MODEL)),
            _resident((D_MODEL, 2 * D_FF)),
            _resident((D_FF, D_MODEL)),
        ],
        out_specs=pl.BlockSpec((tm, D_MODEL), lambda i: (i, 0)),
        out_shape=jax.ShapeDtypeStruct((T, D_MODEL), F32),
        scratch_shapes=[pltpu.VMEM((tm, D_MODEL), F32)],
        compiler_params=_cparams(("arbitrary",)),
        name="ffn",
    )(x, mod, g, w1, w2)


def _head_sumsq(x, ones_bd):
    x2 = x * x
    hi = x2.astype(BF16)
    lo = (x2 - hi.astype(F32)).astype(BF16)
    return jnp.dot(hi, ones_bd, preferred_element_type=F32) + jnp.dot(lo, ones_bd, preferred_element_type=F32)


def _head_rmsnorm(x, gain, ones_bd):
    ss = _head_sumsq(x, ones_bd)
    return x * lax.rsqrt(ss * (1.0 / HEAD_DIM) + EPS) * gain


def _rope(x, cos_t, sin_t):
    n = x.shape[-1]
    lane = lax.broadcasted_iota(jnp.int32, x.shape, 1)
    first = jnp.bitwise_and(lane, 31) < 16
    partner = jnp.where(first, pltpu.roll(x, n - 16, axis=1), pltpu.roll(x, 16, axis=1))
    return x * cos_t + partner * sin_t


def _dup_halves(x):
    lane = lax.broadcasted_iota(jnp.int32, x.shape, 1)
    swapped = pltpu.roll(x, HEAD_DIM, axis=1)
    lo = lane < HEAD_DIM
    return jnp.concatenate([jnp.where(lo, x, swapped), jnp.where(lo, swapped, x)], axis=1)


def _proj_kernel(*refs, latent):
    if latent:
        (x_ref, mod_ref, g_ref, w_ref, qn_ref, kn_ref, oq_ref, ok_ref, cos_ref, sin_ref,
         naq_ref, nak_ref, nav_ref, u_ref, gq_ref, gk_ref, gv_ref, lx_ref) = refs
    else:
        (x_ref, mod_ref, g_ref, w_ref, qn_ref, kn_ref, oq_ref, ok_ref,
         naq_ref, nak_ref, nav_ref, u_ref, gq_ref, gk_ref, gv_ref, lx_ref) = refs
    x = x_ref[...]
    h = _modulate(x, g_ref[2:3], mod_ref[0], 1).astype(BF16)

    def proj(lo, n):
        return jnp.dot(h, w_ref[:, lo:lo + n], preferred_element_type=F32)

    naq_ref[...] = (proj(OFF_NA, NA_W) * ATTN_SCALE).astype(BF16)
    nak_ref[...] = proj(OFF_NA + NA_W, NA_W).astype(nak_ref.dtype)
    nav_ref[...] = proj(OFF_NA + 2 * NA_W, NA_W).astype(nav_ref.dtype)
    glu_a = proj(OFF_CONV, CONV_C)
    glu_g = proj(OFF_CONV + CONV_C, CONV_C)
    u_ref[...] = glu_a * jax.nn.sigmoid(glu_g)
    q = _head_rmsnorm(proj(OFF_GQA, GQA_QW), qn_ref[...], oq_ref[...])
    k = _head_rmsnorm(proj(OFF_GQA + GQA_QW, GQA_KVW), kn_ref[...], ok_ref[...])
    v = proj(OFF_GQA + GQA_QW + GQA_KVW, GQA_KVW)
    if latent:
        cos_t = cos_ref[...]
        sin_t = sin_ref[...]
        q = _rope(q, jnp.concatenate([cos_t] * (GQA_QW // LANES), axis=1),
                  jnp.concatenate([sin_t] * (GQA_QW // LANES), axis=1))
        k = _rope(k, cos_t, sin_t)
        gk_ref[...] = _dup_halves(k).astype(BF16)
        gv_ref[...] = _dup_halves(v).astype(BF16)
    else:
        gk_ref[...] = k
        gv_ref[...] = v
    gq_ref[...] = (q * ATTN_SCALE).astype(BF16)
    lx_ref[...] = proj(OFF_LRU, LRU_W)


def _proj(x, mod, g, w, qn, kn, ones_q, ones_k, rope_tabs, *, latent, tokens_per_mod):
    T = x.shape[0]
    tm = TM_PROJ
    row = lambda n: pl.BlockSpec((tm, n), lambda i: (i, 0))
    in_specs = [
        row(D_MODEL),
        pl.BlockSpec((1, 9, D_MODEL), lambda i: (i * tm // tokens_per_mod, 0, 0)),
        _resident((6, D_MODEL)),
        _resident((D_MODEL, OFF_GATE)),
        _resident((1, GQA_QW)),
        _resident((1, GQA_KVW)),
        _resident((GQA_QW, GQA_QW)),
        _resident((GQA_KVW, GQA_KVW)),
    ]
    args = [x, mod, g, w, qn, kn, ones_q, ones_k]
    kv_dt = BF16 if latent else F32
    gkv_w = 2 * GQA_KVW if latent else GQA_KVW
    if latent:
        pos_blocks = DEC_SEQ // tm
        in_specs += [pl.BlockSpec((tm, LANES), lambda i: (i % pos_blocks, 0))] * 2
        args += list(rope_tabs)
    out_shape = [
        jax.ShapeDtypeStruct((T, NA_W), BF16),
        jax.ShapeDtypeStruct((T, NA_W), kv_dt),
        jax.ShapeDtypeStruct((T, NA_W), kv_dt),
        jax.ShapeDtypeStruct((T, CONV_C), F32),
        jax.ShapeDtypeStruct((T, GQA_QW), BF16),
        jax.ShapeDtypeStruct((T, gkv_w), kv_dt),
        jax.ShapeDtypeStruct((T, gkv_w), kv_dt),
        jax.ShapeDtypeStruct((T, LRU_W), F32),
    ]
    out_specs = [row(s.shape[1]) for s in out_shape]
    return pl.pallas_call(
        functools.partial(_proj_kernel, latent=latent),
        grid=(T // tm,),
        in_specs=in_specs,
        out_specs=out_specs,
        out_shape=out_shape,
        compiler_params=_cparams(("arbitrary",)),
        name="proj_latent" if latent else "proj_context",
    )(*args)


def _lane_lo(shape):
    return lax.broadcasted_iota(jnp.int32, shape, 1) < HEAD_DIM


def _qk(q, k):
    return lax.dot_general(q, k, (((1,), (1,)), ((), ())), preferred_element_type=F32)


def _softmax_pv(score_blocks, value_blocks):
    m = functools.reduce(jnp.maximum, [jnp.max(s, axis=-1, keepdims=True) for s in score_blocks])
    ps = [jnp.exp(s - m) for s in score_blocks]
    l = functools.reduce(jnp.add, [jnp.sum(p, axis=-1, keepdims=True) for p in ps])
    o = functools.reduce(jnp.add, [jnp.dot(p.astype(BF16), v, preferred_element_type=F32)
                                   for p, v in zip(ps, value_blocks)])
    return o / l


def _pair_attention(q2, keys, values, bias=None):
    lo = _lane_lo(q2.shape)
    zero = jnp.zeros_like(q2)
    outs = []
    for half in range(2):
        qm = jnp.where(lo if half == 0 else jnp.logical_not(lo), q2, zero)
        scores = [_qk(qm, k) for k in keys]
        if bias is not None:
            scores[0] = scores[0] + bias[half]
        outs.append(_softmax_pv(scores, values))
    return jnp.where(_lane_lo(outs[0].shape), outs[0], outs[1])


def _ctx_attn_kernel(naq_ref, nak_ref, nav_ref, gq_ref, gk_ref, gv_ref, ona_ref, ogq_ref):
    gk = _dup_halves(gk_ref[...]).astype(BF16)
    gv = _dup_halves(gv_ref[...]).astype(BF16)
    for j in range(HEAD_PAIRS):
        sl = slice(j * LANES, (j + 1) * LANES)
        ona_ref[:, sl] = _pair_attention(
            naq_ref[:, sl], [nak_ref[:, sl].astype(BF16)], [nav_ref[:, sl].astype(BF16)]).astype(BF16)
        kvs = slice((j // 2) * LANES, (j // 2 + 1) * LANES)
        ogq_ref[:, sl] = _pair_attention(gq_ref[:, sl], [gk[:, kvs]], [gv[:, kvs]]).astype(BF16)


def _ctx_attn(naq, nak, nav, gq, gk, gv):
    T = naq.shape[0]
    row = lambda n: pl.BlockSpec((SEQ, n), lambda b: (b, 0))
    return pl.pallas_call(
        _ctx_attn_kernel,
        grid=(T // SEQ,),
        in_specs=[row(NA_W), row(NA_W), row(NA_W), row(GQA_QW), row(GQA_KVW), row(GQA_KVW)],
        out_specs=[row(NA_W), row(GQA_QW)],
        out_shape=[jax.ShapeDtypeStruct((T, NA_W), BF16), jax.ShapeDtypeStruct((T, GQA_QW), BF16)],
        compiler_params=_cparams(("arbitrary",)),
        name="ctx_attn",
    )(naq, nak, nav, gq, gk, gv)


def _na_row_start(r):
    return min(max(r - NA_WIN_R // 2, 0), GRID_ROWS - NA_WIN_R)


def _na_bias_index(rq, rk):
    r0 = _na_row_start(rq)
    if r0 <= rk < r0 + NA_WIN_R:
        return rk - rq + NA_WIN_R - 1
    return 2 * NA_WIN_R - 1


def _na_latent_kernel(q_ref, k_ref, v_ref, ck_ref, cv_ref, bias_ref, o_ref):
    ck = ck_ref[0, 0].astype(BF16)
    cv = cv_ref[0, 0].astype(BF16)
    qrows = NA_QROWS * GRID_W
    lo = _lane_lo((GRID_W, LANES))
    for qb in range(GRID_ROWS // NA_QROWS):
        w0 = min(max(NA_QROWS * qb - NA_WIN_R // 2, 0), GRID_ROWS - NA_WROWS)
        w0 -= w0 % 2
        kwin = k_ref[w0 * GRID_W:(w0 + NA_WROWS) * GRID_W, :]
        vwin = v_ref[w0 * GRID_W:(w0 + NA_WROWS) * GRID_W, :]
        bias = []
        for half in range(2):
            rows = []
            for qi in range(NA_QROWS):
                rq = NA_QROWS * qb + qi
                blocks = []
                for kp in range(NA_WROWS // 2):
                    ia = _na_bias_index(rq, w0 + 2 * kp)
                    ib = _na_bias_index(rq, w0 + 2 * kp + 1)
                    blocks.append(jnp.where(lo, bias_ref[half, ia], bias_ref[half, ib]))
                rows.append(jnp.concatenate(blocks, axis=1))
            bias.append(jnp.concatenate(rows, axis=0))
        q2 = q_ref[qb * qrows:(qb + 1) * qrows, :]
        o_ref[qb * qrows:(qb + 1) * qrows, :] = _pair_attention(q2, [kwin, ck], [vwin, cv], bias).astype(BF16)


def _na_latent(q, k, v, cache_k, cache_v, bias_tab, layer):
    tok = pl.BlockSpec((DEC_SEQ, LANES), lambda b, j: (b, j))
    cache = pl.BlockSpec((1, 1, PAST_LEN, LANES), lambda b, j: (b, layer, 0, j))
    return pl.pallas_call(
        _na_latent_kernel,
        grid=(DEC_BATCH, HEAD_PAIRS),
        in_specs=[tok, tok, tok, cache, cache,
                  pl.BlockSpec((2, 2 * NA_WIN_R, GRID_W, LANES), lambda b, j: (j, 0, 0, 0))],
        out_specs=tok,
        out_shape=jax.ShapeDtypeStruct(q.shape, BF16),
        compiler_params=_cparams(("arbitrary", "arbitrary")),
        name="na_latent",
    )(q, k, v, cache_k, cache_v, bias_tab)


def _gqa_latent_kernel(q_ref, k_ref, v_ref, ck_ref, cv_ref, o_ref):
    group = pl.program_id(1) // (HEAD_PAIRS // GQA_KV_HEADS)

    def both_halves(x):
        lane = lax.broadcasted_iota(jnp.int32, x.shape, 1)
        keep = jnp.where(lane < HEAD_DIM, 0, 1) == group
        return jnp.where(keep, x, pltpu.roll(x, HEAD_DIM, axis=1)).astype(BF16)

    ck = both_halves(ck_ref[0, 0])
    cv = both_halves(cv_ref[0, 0])
    k = k_ref[...]
    v = v_ref[...]
    for qb in range(DEC_SEQ // GQA_QBLK):
        sl = slice(qb * GQA_QBLK, (qb + 1) * GQA_QBLK)
        o_ref[sl, :] = _pair_attention(q_ref[sl, :], [k, ck], [v, cv]).astype(BF16)


def _gqa_latent(q, k, v, cache_k, cache_v, layer):
    tok = pl.BlockSpec((DEC_SEQ, LANES), lambda b, j: (b, j))
    kv = pl.BlockSpec((DEC_SEQ, LANES), lambda b, j: (b, j // (HEAD_PAIRS // GQA_KV_HEADS)))
    cache = pl.BlockSpec((1, 1, PAST_LEN, GQA_KVW), lambda b, j: (b, layer, 0, 0))
    return pl.pallas_call(
        _gqa_latent_kernel,
        grid=(DEC_BATCH, HEAD_PAIRS),
        in_specs=[tok, kv, kv, cache, cache],
        out_specs=tok,
        out_shape=jax.ShapeDtypeStruct(q.shape, BF16),
        compiler_params=_cparams(("arbitrary", "arbitrary")),
        name="gqa_latent",
    )(q, k, v, cache_k, cache_v)


def _conv_kernel(u_ref, dw_ref, g_ref, b_ref, o_ref, pad_ref, *, L):
    zeros = jnp.zeros((CONV_PAD, CONV_C), F32)
    pad_ref[0:CONV_PAD, :] = zeros
    pad_ref[CONV_PAD + L:2 * CONV_PAD + L, :] = zeros
    pad_ref[CONV_PAD:CONV_PAD + L, :] = u_ref[...]
    first = CONV_PAD - CONV_W // 2

    for c in range(L // CONV_ROWS):
        base = c * CONV_ROWS
        cols = []
        for lc in range(CONV_C // LANES):
            ls = slice(lc * LANES, (lc + 1) * LANES)
            acc = jnp.zeros((CONV_ROWS, LANES), F32)
            for k in range(CONV_W):
                acc = acc + dw_ref[k:k + 1, ls] * pad_ref[base + first + k:base + first + k + CONV_ROWS, ls]
            cols.append(acc)
        acc = jnp.concatenate(cols, axis=1)
        mu = jnp.mean(acc, axis=-1, keepdims=True)
        d = acc - mu
        var = jnp.mean(d * d, axis=-1, keepdims=True)
        y = d * lax.rsqrt(var + EPS) * g_ref[...] + b_ref[...]
        o_ref[base:base + CONV_ROWS, :] = (y * jax.nn.sigmoid(y)).astype(BF16)


def _conv_branch(u, dw, ln_g, ln_b, *, L):
    T = u.shape[0]
    return pl.pallas_call(
        functools.partial(_conv_kernel, L=L),
        grid=(T // L,),
        in_specs=[pl.BlockSpec((L, CONV_C), lambda b: (b, 0)), _resident((CONV_W, CONV_C)),
                  _resident((1, CONV_C)), _resident((1, CONV_C))],
        out_specs=pl.BlockSpec((L, CONV_C), lambda b: (b, 0)),
        out_shape=jax.ShapeDtypeStruct((T, CONV_C), BF16),
        scratch_shapes=[pltpu.VMEM((L + 2 * CONV_PAD, CONV_C), F32)],
        compiler_params=_cparams(("arbitrary",)),
        name="conv_branch",
    )(u, dw, ln_g, ln_b)


def _lru_kernel(x_ref, cw_ref, cb_ref, wg_ref, bg_ref, lam_ref, h0_ref, y_ref, st_ref,
                pad_ref, a_ref, b_ref, hf_ref, hb_ref, *, L):
    zeros = jnp.zeros((LRU_PAD, LRU_W), F32)
    pad_ref[0:LRU_PAD, :] = zeros
    pad_ref[LRU_PAD + L:2 * LRU_PAD + L, :] = zeros
    pad_ref[LRU_PAD:LRU_PAD + L, :] = x_ref[...]
    first = LRU_PAD - LRU_CONV_W // 2
    decay = LRU_C * jax.nn.log_sigmoid(lam_ref[...])
    for c in range(L // LRU_GATE_ROWS):
        r0 = c * LRU_GATE_ROWS
        xc = cb_ref[...] + functools.reduce(jnp.add, [
            cw_ref[k:k + 1, :] * pad_ref[r0 + first + k:r0 + first + k + LRU_GATE_ROWS, :]
            for k in range(LRU_CONV_W)])
        xcb = xc.astype(BF16)

        def gate(d, which):
            z = jnp.concatenate([
                jnp.dot(xcb[:, :LRU_HALF], wg_ref[d, which, 0], preferred_element_type=F32),
                jnp.dot(xcb[:, LRU_HALF:], wg_ref[d, which, 1], preferred_element_type=F32)], axis=1)
            return jax.nn.sigmoid(z + bg_ref[d, which:which + 1, :])

        for d in range(2):
            log_a = decay[d:d + 1, :] * gate(d, 0)
            a = jnp.exp(log_a)
            one_minus_a2 = -jnp.tanh(log_a) * (a * a + 1.0)
            a_ref[d, r0:r0 + LRU_GATE_ROWS, :] = a
            b_ref[d, r0:r0 + LRU_GATE_ROWS, :] = jnp.sqrt(one_minus_a2) * (gate(d, 1) * xc)

    def step(t, carry):
        hf, hb = carry
        tb = L - 1 - t
        hf = a_ref[0, pl.ds(t, 1), :] * hf + b_ref[0, pl.ds(t, 1), :]
        hb = a_ref[1, pl.ds(tb, 1), :] * hb + b_ref[1, pl.ds(tb, 1), :]
        hf_ref[pl.ds(t, 1), :] = hf
        hb_ref[pl.ds(tb, 1), :] = hb
        return hf, hb

    h0 = h0_ref[0]
    hf, hb = lax.fori_loop(0, L, step, (h0[0:1, :], h0[1:2, :]), unroll=8)
    y_ref[...] = (hf_ref[...] + hb_ref[...]).astype(BF16)
    st_ref[0] = jnp.concatenate([hf, hb], axis=0)


def _lru_branch(x, cw, cb, wg, bg, lam, h0, *, L):
    T = x.shape[0]
    nb = T // L
    return pl.pallas_call(
        functools.partial(_lru_kernel, L=L),
        grid=(nb,),
        in_specs=[pl.BlockSpec((L, LRU_W), lambda b: (b, 0)), _resident((LRU_CONV_W, LRU_W)),
                  _resident((1, LRU_W)), _resident((2, 2, 2, LRU_HALF, LRU_HALF)),
                  _resident((2, 2, LRU_W)), _resident((2, LRU_W)),
                  pl.BlockSpec((1, 2, LRU_W), lambda b: (b, 0, 0))],
        out_specs=[pl.BlockSpec((L, LRU_W), lambda b: (b, 0)), pl.BlockSpec((1, 2, LRU_W), lambda b: (b, 0, 0))],
        out_shape=[jax.ShapeDtypeStruct((T, LRU_W), BF16), jax.ShapeDtypeStruct((nb, 2, LRU_W), F32)],
        scratch_shapes=[pltpu.VMEM((L + 2 * LRU_PAD, LRU_W), F32), pltpu.VMEM((2, L, LRU_W), F32),
                        pltpu.VMEM((2, L, LRU_W), F32), pltpu.VMEM((L, LRU_W), F32), pltpu.VMEM((L, LRU_W), F32)],
        compiler_params=_cparams(("arbitrary",)),
        name="lru_branch",
    )(x, cw, cb, wg, bg, lam, h0)


def _merge_kernel(x_ref, mod_ref, g_ref, wg_ref, b0_ref, b1_ref, b2_ref, b3_ref, wb_ref, wo_ref, o_ref):
    x = x_ref[...]
    mod = mod_ref[0]
    h = _modulate(x, g_ref[2:3], mod, 1).astype(BF16)
    merged = None
    for k, br in enumerate((b0_ref, b1_ref, b2_ref, b3_ref)):
        logits = jnp.dot(h, wg_ref[:, k * D_MODEL:(k + 1) * D_MODEL], preferred_element_type=F32)
        term = jax.nn.sigmoid(logits) * jnp.dot(br[...], wb_ref[k], preferred_element_type=F32)
        merged = term if merged is None else merged + term
    y = jnp.dot(merged.astype(BF16), wo_ref[...], preferred_element_type=F32)
    o_ref[...] = x + mod[5:6] * (_rms(y) * g_ref[3:4])


def _merge(x, mod, g, w_gate, branches, w_branch, w_out, *, tokens_per_mod):
    T = x.shape[0]
    tm = TM_MERGE
    row = lambda n: pl.BlockSpec((tm, n), lambda i: (i, 0))
    return pl.pallas_call(
        _merge_kernel,
        grid=(T // tm,),
        in_specs=[row(D_MODEL), pl.BlockSpec((1, 9, D_MODEL), lambda i: (i * tm // tokens_per_mod, 0, 0)),
                  _resident((6, D_MODEL)), _resident((D_MODEL, N_BRANCH * D_MODEL)),
                  row(BRANCH_W), row(BRANCH_W), row(BRANCH_W), row(BRANCH_W),
                  _resident((N_BRANCH, BRANCH_W, D_MODEL)), _resident((D_MODEL, D_MODEL))],
        out_specs=row(D_MODEL),
        out_shape=jax.ShapeDtypeStruct((T, D_MODEL), F32),
        compiler_params=_cparams(("arbitrary",)),
        name="merge",
    )(x, mod, g, w_gate, *branches, w_branch, w_out)


def _rope_tables():
    t = np.arange(DEC_SEQ)
    row = (t // GRID_W).astype(np.float32)
    col = (t % GRID_W).astype(np.float32)
    half = HEAD_DIM // 2
    freqs = (np.float32(ROPE_THETA) ** (-np.arange(0, half, 2, dtype=np.float32) / np.float32(half))).astype(np.float32)
    ar = row[:, None] * freqs
    ac = col[:, None] * freqs
    cos_h = np.concatenate([np.cos(ar), np.cos(ar), np.cos(ac), np.cos(ac)], axis=1)
    sin_h = np.concatenate([-np.sin(ar), np.sin(ar), -np.sin(ac), np.sin(ac)], axis=1)
    return (jnp.asarray(np.tile(cos_h, (1, 2)), F32), jnp.asarray(np.tile(sin_h, (1, 2)), F32))


def _na_bias_tables(na_rpb):
    c = np.arange(GRID_W)
    c0 = np.clip(c - NA_WIN_C // 2, 0, GRID_W - NA_WIN_C)
    col_ok = (c[None, :] >= c0[:, None]) & (c[None, :] < c0[:, None] + NA_WIN_C)
    dc = np.clip(c[None, :] - c[:, None], -(NA_WIN_C - 1), NA_WIN_C - 1) + NA_WIN_C - 1
    tab = jnp.where(col_ok, na_rpb[..., dc].astype(F32), NEG_INF)
    masked = jnp.full(tab.shape[:2] + (1, GRID_W, GRID_W), NEG_INF, F32)
    tab = jnp.concatenate([tab, masked], axis=2)
    return jnp.concatenate([tab, tab], axis=-1)


def _block_diag_ones(n):
    return jnp.asarray(np.kron(np.eye(n // HEAD_DIM), np.ones((HEAD_DIM, HEAD_DIM))), BF16)


def _lru_gate_weights(lru_wr, lru_wi):
    w = jnp.stack([lru_wr, lru_wi], axis=2)
    per_half = LRU_BLOCKS // 2
    bw = LRU_W // LRU_BLOCKS
    w = w.reshape(DEPTH, 2, 2, 2, per_half, bw, bw)
    eye = jnp.asarray(np.eye(per_half), F32)
    bd = jnp.einsum('...nij,nm->...nimj', w, eye).reshape(DEPTH, 2, 2, 2, LRU_HALF, LRU_HALF)
    return bd.astype(BF16)


def kernel(x_prompt, x_sample, c, cache_na_k, cache_na_v, cache_gqa_k, cache_gqa_v, state_lru, c_ctx, w_ada, b_ada, norm_g, ffn_w1, ffn_w2, w_in, na_rpb, conv_dw, conv_ln_g, conv_ln_b, gqa_q_norm, gqa_k_norm, lru_conv_w, lru_conv_b, lru_wr, lru_br, lru_wi, lru_bi, lru_lambda, w_branch, w_out):
    tp = BATCH * SEQ
    ts = DEC_BATCH * DEC_SEQ
    c_all = jnp.concatenate([c_ctx[None, :], c, jnp.zeros((ADA_ROWS - 1 - DEC_BATCH, D_MODEL), F32)], axis=0)
    mods = _adaln(c_all, w_ada, b_ada).reshape(DEPTH, ADA_ROWS, 9, D_MODEL)

    w1 = ffn_w1.astype(BF16)
    w2 = ffn_w2.astype(BF16)
    w_proj = w_in[:, :, :OFF_GATE].astype(BF16)
    w_gate = w_in[:, :, OFF_GATE:].astype(BF16)
    wb = w_branch.astype(BF16)
    wo = w_out.astype(BF16)
    lru_wg = _lru_gate_weights(lru_wr, lru_wi)
    lru_bg = jnp.stack([lru_br, lru_bi], axis=2)
    bias_tab = _na_bias_tables(na_rpb)
    rope_tabs = _rope_tables()
    ones_q = _block_diag_ones(GQA_QW)
    ones_k = _block_diag_ones(GQA_KVW)
    cna_k = cache_na_k.reshape(DEC_BATCH, DEPTH, PAST_LEN, NA_W)
    cna_v = cache_na_v.reshape(DEC_BATCH, DEPTH, PAST_LEN, NA_W)
    cgq_k = cache_gqa_k.reshape(DEC_BATCH, DEPTH, PAST_LEN, GQA_KVW)
    cgq_v = cache_gqa_v.reshape(DEC_BATCH, DEPTH, PAST_LEN, GQA_KVW)
    h0_prompt = jnp.zeros((BATCH, 2, LRU_W), F32)

    xp = x_prompt.reshape(tp, D_MODEL)
    xs = x_sample.reshape(ts, D_MODEL)
    nk_l, nv_l, gk_l, gv_l, st_l = [], [], [], [], []
    for l in range(DEPTH):
        g = norm_g[l]
        mod_p = mods[l, 0:1]
        mod_s = mods[l, 1:1 + DEC_BATCH]
        qn = jnp.tile(gqa_q_norm[l], GQA_HEADS)[None, :]
        kn = jnp.tile(gqa_k_norm[l], GQA_KV_HEADS)[None, :]
        lru_args = (lru_conv_w[l], lru_conv_b[l][None, :], lru_wg[l], lru_bg[l], lru_lambda[l])
        conv_args = (conv_dw[l], conv_ln_g[l][None, :], conv_ln_b[l][None, :])

        xp = _ffn(xp, mod_p, g, w1[l, 0], w2[l, 0], sub=0, tokens_per_mod=tp)
        naq, nak, nav, u, gq, gk, gv, lx = _proj(xp, mod_p, g, w_proj[l], qn, kn, ones_q, ones_k, None,
                                                  latent=False, tokens_per_mod=tp)
        o_na, o_gq = _ctx_attn(naq, nak, nav, gq, gk, gv)
        o_conv = _conv_branch(u, *conv_args, L=SEQ)
        o_lru, st = _lru_branch(lx, *lru_args, h0_prompt, L=SEQ)
        xp = _merge(xp, mod_p, g, w_gate[l], (o_na, o_conv, o_gq, o_lru), wb[l], wo[l], tokens_per_mod=tp)
        xp = _ffn(xp, mod_p, g, w1[l, 1], w2[l, 1], sub=2, tokens_per_mod=tp)
        nk_l.append(nak.reshape(BATCH, SEQ, NA_HEADS, HEAD_DIM))
        nv_l.append(nav.reshape(BATCH, SEQ, NA_HEADS, HEAD_DIM))
        gk_l.append(gk.reshape(BATCH, SEQ, GQA_KV_HEADS, HEAD_DIM))
        gv_l.append(gv.reshape(BATCH, SEQ, GQA_KV_HEADS, HEAD_DIM))
        st_l.append(st)

        xs = _ffn(xs, mod_s, g, w1[l, 0], w2[l, 0], sub=0, tokens_per_mod=DEC_SEQ)
        naq, nak, nav, u, gq, gk, gv, lx = _proj(xs, mod_s, g, w_proj[l], qn, kn, ones_q, ones_k, rope_tabs,
                                                  latent=True, tokens_per_mod=DEC_SEQ)
        o_na = _na_latent(naq, nak, nav, cna_k, cna_v, bias_tab[l], l)
        o_gq = _gqa_latent(gq, gk, gv, cgq_k, cgq_v, l)
        o_conv = _conv_branch(u, *conv_args, L=DEC_SEQ)
        o_lru, _ = _lru_branch(lx, *lru_args, state_lru[:, l], L=DEC_SEQ)
        xs = _merge(xs, mod_s, g, w_gate[l], (o_na, o_conv, o_gq, o_lru), wb[l], wo[l], tokens_per_mod=DEC_SEQ)
        xs = _ffn(xs, mod_s, g, w1[l, 1], w2[l, 1], sub=2, tokens_per_mod=DEC_SEQ)

    return (xp.reshape(BATCH, SEQ, D_MODEL), xs.reshape(DEC_BATCH, DEC_SEQ, D_MODEL),
            jnp.stack(nk_l, axis=1), jnp.stack(nv_l, axis=1), jnp.stack(gk_l, axis=1), jnp.stack(gv_l, axis=1),
            jnp.stack(st_l, axis=1))
```

```python
import functools

import numpy as np
import jax
import jax.numpy as jnp
from jax import lax
from jax.experimental import pallas as pl
from jax.experimental.pallas import tpu as pltpu

D_MODEL = 1024
BATCH = 16
SEQ = 256
DEPTH = 4
DEC_BATCH = 8
DEC_SEQ = 1024
PAST_LEN = 512
GRID_W = 64
GRID_ROWS = DEC_SEQ // GRID_W
HEAD_DIM = 64
NA_HEADS = 8
NA_WIN_R = 8
NA_WIN_C = 16
CONV_C = 512
CONV_W = 31
GQA_HEADS = 8
GQA_KV_HEADS = 2
LRU_W = 512
LRU_BLOCKS = 8
LRU_CONV_W = 4
LRU_C = 8.0
N_BRANCH = 4
BRANCH_W = 512
D_FF = 2816
ROPE_THETA = 10000.0
EPS = 1e-6
NEG_INF = -1e30

NA_W = NA_HEADS * HEAD_DIM
GQA_QW = GQA_HEADS * HEAD_DIM
GQA_KVW = GQA_KV_HEADS * HEAD_DIM
OFF_NA = 0
OFF_CONV = OFF_NA + 3 * NA_W
OFF_GQA = OFF_CONV + 2 * CONV_C
OFF_LRU = OFF_GQA + GQA_QW + 2 * GQA_KVW
OFF_GATE = OFF_LRU + LRU_W
IN_COLS = OFF_GATE + N_BRANCH * D_MODEL
N_MOD = 9 * D_MODEL

F32 = jnp.float32
BF16 = jnp.bfloat16

LANES = 128
SUBLANES = 8
HEAD_PAIRS = NA_HEADS // 2
LOG2E = 1.4426950408889634
ATTN_SCALE = HEAD_DIM ** -0.5 * LOG2E
ADA_ROWS = 16
VMEM_LIMIT = 56 * 1024 * 1024

TM_FFN = 512
TM_PROJ = 256
TM_MERGE = 256
FF_CHUNK = 256
ADA_TN = 1152
CONV_ROWS = 128
CONV_PAD = 16
LRU_PAD = 8
LRU_GATE_ROWS = 256
LRU_HALF = LRU_W // 2
LRU_NB = 2
NA_QROWS = 2
NA_WROWS = 10
GQA_QBLK = 256
CTX_LAG = 2


def _cparams(sem):
    return pltpu.CompilerParams(dimension_semantics=sem, vmem_limit_bytes=VMEM_LIMIT)


def _rms(x):
    return x * lax.rsqrt(jnp.mean(x * x, axis=-1, keepdims=True) + EPS)


def _modulate(x, g_pre, mod, sub):
    shift = mod[3 * sub:3 * sub + 1]
    scale = mod[3 * sub + 1:3 * sub + 2]
    return (_rms(x) * g_pre) * (1.0 + scale) + shift


def _stacked(idx, shape):
    idx = tuple(idx)
    return pl.BlockSpec((None,) * len(idx) + tuple(shape), lambda *_: idx + (0,) * len(shape),
                        pipeline_mode=pl.Buffered(1))


def _resident(shape):
    return _stacked((), shape)


def _adaln_kernel(c_ref, w_ref, b_ref, o_ref):
    c = c_ref[...]
    s = (c * jax.nn.sigmoid(c)).astype(BF16)
    o_ref[0] = jnp.dot(s, w_ref[0].astype(BF16), preferred_element_type=F32) + b_ref[0]


def _adaln(c_all, w_ada, b_ada):
    return pl.pallas_call(
        _adaln_kernel,
        grid=(DEPTH, N_MOD // ADA_TN),
        in_specs=[
            pl.BlockSpec((ADA_ROWS, D_MODEL), lambda l, n: (0, 0)),
            pl.BlockSpec((1, D_MODEL, ADA_TN), lambda l, n: (l, 0, n)),
            pl.BlockSpec((1, 1, ADA_TN), lambda l, n: (l, 0, n)),
        ],
        out_specs=pl.BlockSpec((1, ADA_ROWS, ADA_TN), lambda l, n: (l, 0, n)),
        out_shape=jax.ShapeDtypeStruct((DEPTH, ADA_ROWS, N_MOD), F32),
        compiler_params=_cparams(("arbitrary", "arbitrary")),
        name="adaln",
    )(c_all, w_ada, b_ada.reshape(DEPTH, 1, N_MOD))


def _ffn_kernel(x_ref, mod_ref, g_ref, w1_ref, w2_ref, o_ref, acc_ref, *, sub, res_w):
    x = x_ref[...]
    mod = mod_ref[0]
    h = _modulate(x, g_ref[2 * sub:2 * sub + 1], mod, sub).astype(BF16)
    for c in range(D_FF // FF_CHUNK):
        lo = c * FF_CHUNK
        a = jnp.dot(h, w1_ref[:, lo:lo + FF_CHUNK], preferred_element_type=F32)
        u = jnp.dot(h, w1_ref[:, D_FF + lo:D_FF + lo + FF_CHUNK], preferred_element_type=F32)
        act = (a * jax.nn.sigmoid(a) * u).astype(BF16)
        part = jnp.dot(act, w2_ref[lo:lo + FF_CHUNK, :], preferred_element_type=F32)
        if c == 0:
            acc_ref[...] = part
        else:
            acc_ref[...] += part
    gate = mod[3 * sub + 2:3 * sub + 3]
    o_ref[...] = x + (res_w * gate) * (_rms(acc_ref[...]) * g_ref[2 * sub + 1:2 * sub + 2])


def _ffn(x, mod, g, w1, w2, *, layer, slot, tokens_per_mod):
    T = x.shape[0]
    tm = TM_FFN
    return pl.pallas_call(
        functools.partial(_ffn_kernel, sub=2 * slot, res_w=0.5),
        grid=(T // tm,),
        in_specs=[
            pl.BlockSpec((tm, D_MODEL), lambda i: (i, 0)),
            pl.BlockSpec((1, 9, D_MODEL), lambda i: (i * tm // tokens_per_mod, 0, 0)),
            _resident((6, D_MODEL)),
            _stacked((layer, slot), (D_MODEL, 2 * D_FF)),
            _stacked((layer, slot), (D_FF, D_MODEL)),
        ],
        out_specs=pl.BlockSpec((tm, D_MODEL), lambda i: (i, 0)),
        out_shape=jax.ShapeDtypeStruct((T, D_MODEL), F32),
        scratch_shapes=[pltpu.VMEM((tm, D_MODEL), F32)],
        compiler_params=_cparams(("arbitrary",)),
        name="ffn",
    )(x, mod, g, w1, w2)


def _head_sumsq(x, ones_bd):
    x2 = x * x
    hi = x2.astype(BF16)
    lo = (x2 - hi.astype(F32)).astype(BF16)
    return jnp.dot(hi, ones_bd, preferred_element_type=F32) + jnp.dot(lo, ones_bd, preferred_element_type=F32)


def _head_rmsnorm(x, gain, ones_bd):
    ss = _head_sumsq(x, ones_bd)
    return x * lax.rsqrt(ss * (1.0 / HEAD_DIM) + EPS) * gain


def _rope(x, cos_t, sin_t):
    n = x.shape[-1]
    lane = lax.broadcasted_iota(jnp.int32, x.shape, 1)
    first = jnp.bitwise_and(lane, 31) < 16
    partner = jnp.where(first, pltpu.roll(x, n - 16, axis=1), pltpu.roll(x, 16, axis=1))
    return x * cos_t + partner * sin_t


def _dup_halves(x):
    lane = lax.broadcasted_iota(jnp.int32, x.shape, 1)
    swapped = pltpu.roll(x, HEAD_DIM, axis=1)
    lo = lane < HEAD_DIM
    return jnp.concatenate([jnp.where(lo, x, swapped), jnp.where(lo, swapped, x)], axis=1)


def _proj_kernel(*refs, latent):
    if latent:
        (x_ref, mod_ref, g_ref, w_ref, qn_ref, kn_ref, oq_ref, ok_ref, cos_ref, sin_ref,
         naq_ref, nak_ref, nav_ref, u_ref, gq_ref, gk_ref, gv_ref, lx_ref) = refs
    else:
        (x_ref, mod_ref, g_ref, w_ref, qn_ref, kn_ref, oq_ref, ok_ref,
         naq_ref, nak_ref, nav_ref, u_ref, gq_ref, gk_ref, gv_ref, lx_ref) = refs
    x = x_ref[...]
    h = _modulate(x, g_ref[2:3], mod_ref[0], 1).astype(BF16)

    def proj(lo, n):
        return jnp.dot(h, w_ref[:, lo:lo + n], preferred_element_type=F32)

    naq_ref[...] = (proj(OFF_NA, NA_W) * ATTN_SCALE).astype(BF16)
    nak_ref[...] = proj(OFF_NA + NA_W, NA_W).astype(nak_ref.dtype)
    nav_ref[...] = proj(OFF_NA + 2 * NA_W, NA_W).astype(nav_ref.dtype)
    glu_a = proj(OFF_CONV, CONV_C)
    glu_g = proj(OFF_CONV + CONV_C, CONV_C)
    u_ref[...] = glu_a * jax.nn.sigmoid(glu_g)
    q = _head_rmsnorm(proj(OFF_GQA, GQA_QW), qn_ref[...], oq_ref[...])
    k = _head_rmsnorm(proj(OFF_GQA + GQA_QW, GQA_KVW), kn_ref[...], ok_ref[...])
    v = proj(OFF_GQA + GQA_QW + GQA_KVW, GQA_KVW)
    if latent:
        cos_t = cos_ref[...]
        sin_t = sin_ref[...]
        q = _rope(q, jnp.concatenate([cos_t] * (GQA_QW // LANES), axis=1),
                  jnp.concatenate([sin_t] * (GQA_QW // LANES), axis=1))
        k = _rope(k, cos_t, sin_t)
        gk_ref[...] = _dup_halves(k).astype(BF16)
        gv_ref[...] = _dup_halves(v).astype(BF16)
    else:
        gk_ref[...] = k
        gv_ref[...] = v
    gq_ref[...] = (q * ATTN_SCALE).astype(BF16)
    lx_ref[...] = proj(OFF_LRU, LRU_W)


def _proj(x, mod, g, w, qn, kn, ones_q, ones_k, rope_tabs, *, layer, latent, tokens_per_mod):
    T = x.shape[0]
    tm = TM_PROJ
    row = lambda n: pl.BlockSpec((tm, n), lambda i: (i, 0))
    in_specs = [
        row(D_MODEL),
        pl.BlockSpec((1, 9, D_MODEL), lambda i: (i * tm // tokens_per_mod, 0, 0)),
        _resident((6, D_MODEL)),
        _stacked((layer,), (D_MODEL, OFF_GATE)),
        _resident((1, GQA_QW)),
        _resident((1, GQA_KVW)),
        _resident((GQA_QW, GQA_QW)),
        _resident((GQA_KVW, GQA_KVW)),
    ]
    args = [x, mod, g, w, qn, kn, ones_q, ones_k]
    kv_dt = BF16 if latent else F32
    gkv_w = 2 * GQA_KVW if latent else GQA_KVW
    if latent:
        pos_blocks = DEC_SEQ // tm
        in_specs += [pl.BlockSpec((tm, LANES), lambda i: (i % pos_blocks, 0))] * 2
        args += list(rope_tabs)
    out_shape = [
        jax.ShapeDtypeStruct((T, NA_W), BF16),
        jax.ShapeDtypeStruct((T, NA_W), kv_dt),
        jax.ShapeDtypeStruct((T, NA_W), kv_dt),
        jax.ShapeDtypeStruct((T, CONV_C), F32),
        jax.ShapeDtypeStruct((T, GQA_QW), BF16),
        jax.ShapeDtypeStruct((T, gkv_w), kv_dt),
        jax.ShapeDtypeStruct((T, gkv_w), kv_dt),
        jax.ShapeDtypeStruct((T, LRU_W), F32),
    ]
    out_specs = [row(s.shape[1]) for s in out_shape]
    return pl.pallas_call(
        functools.partial(_proj_kernel, latent=latent),
        grid=(T // tm,),
        in_specs=in_specs,
        out_specs=out_specs,
        out_shape=out_shape,
        compiler_params=_cparams(("arbitrary",)),
        name="proj_latent" if latent else "proj_context",
    )(*args)


def _lane_lo(shape):
    return lax.broadcasted_iota(jnp.int32, shape, 1) < HEAD_DIM


def _qk(q, k):
    return lax.dot_general(q, k, (((1,), (1,)), ((), ())), preferred_element_type=F32)


def _values_with_ones(v):
    lo = _lane_lo(v.shape)
    ones = jnp.ones_like(v)
    return jnp.where(lo, v, ones), jnp.where(lo, ones, v)


def _pair_attention_pipeline(work, lag=1):
    heads = [(item, half) for item in work for half in range(2)]

    def scores_of(item, half):
        load_q, keys, _, bias_fn, _ = item
        q2 = load_q()
        lo = _lane_lo(q2.shape)
        qm = jnp.where(lo if half == 0 else jnp.logical_not(lo), q2, jnp.zeros_like(q2))
        scores = [_qk(qm, k) for k in keys]
        if bias_fn is not None:
            scores[0] = scores[0] + bias_fn(half)
        return scores

    def softmax_of(scores):
        m = functools.reduce(jnp.maximum, [jnp.max(s, axis=-1, keepdims=True) for s in scores])
        return [jnp.exp2(s - m).astype(BF16) for s in scores]

    def output_of(item, half, ps):
        return functools.reduce(jnp.add, [jnp.dot(p, v, preferred_element_type=F32)
                                          for p, v in zip(ps, item[2][half])])

    def finish(item, raw0, raw1):
        lo = _lane_lo(raw0.shape)
        denom = pltpu.roll(jnp.where(lo, raw1, raw0), HEAD_DIM, axis=1)
        item[4](jnp.where(lo, raw0, raw1) * (1.0 / denom))

    n = len(heads)
    scores, probs, raw = {}, {}, {}
    for t in range(n + 4 * lag):
        if t < n:
            scores[t] = scores_of(*heads[t])
        if 0 <= t - lag < n:
            probs[t - lag] = softmax_of(scores.pop(t - lag))
        if 0 <= t - 2 * lag < n:
            item, half = heads[t - 2 * lag]
            raw[t - 2 * lag] = output_of(item, half, probs.pop(t - 2 * lag))
        i = t - 2 * lag - 2
        if 0 <= i < n and i % 2 == 0:
            finish(heads[i][0], raw.pop(i), raw.pop(i + 1))


def _ctx_attn_kernel(naq_ref, nak_ref, nav_ref, gq_ref, gk_ref, gv_ref, ona_ref, ogq_ref):
    gk = _dup_halves(gk_ref[...]).astype(BF16)
    gv = _dup_halves(gv_ref[...]).astype(BF16)
    work = []
    for j in range(HEAD_PAIRS):
        sl = slice(j * LANES, (j + 1) * LANES)
        kvs = slice((j // 2) * LANES, (j // 2 + 1) * LANES)

        def store_na(o, sl=sl):
            ona_ref[:, sl] = o.astype(BF16)

        def store_gq(o, sl=sl):
            ogq_ref[:, sl] = o.astype(BF16)

        nav0, nav1 = _values_with_ones(nav_ref[:, sl].astype(BF16))
        gv0, gv1 = _values_with_ones(gv[:, kvs])
        work.append((lambda sl=sl: naq_ref[:, sl], [nak_ref[:, sl].astype(BF16)], ([nav0], [nav1]), None, store_na))
        work.append((lambda sl=sl: gq_ref[:, sl], [gk[:, kvs]], ([gv0], [gv1]), None, store_gq))
    _pair_attention_pipeline(work, lag=CTX_LAG)


def _ctx_attn(naq, nak, nav, gq, gk, gv):
    T = naq.shape[0]
    row = lambda n: pl.BlockSpec((SEQ, n), lambda b: (b, 0))
    return pl.pallas_call(
        _ctx_attn_kernel,
        grid=(T // SEQ,),
        in_specs=[row(NA_W), row(NA_W), row(NA_W), row(GQA_QW), row(GQA_KVW), row(GQA_KVW)],
        out_specs=[row(NA_W), row(GQA_QW)],
        out_shape=[jax.ShapeDtypeStruct((T, NA_W), BF16), jax.ShapeDtypeStruct((T, GQA_QW), BF16)],
        compiler_params=_cparams(("arbitrary",)),
        name="ctx_attn",
    )(naq, nak, nav, gq, gk, gv)


def _na_row_start(r):
    return min(max(r - NA_WIN_R // 2, 0), GRID_ROWS - NA_WIN_R)


def _na_bias_index(rq, rk):
    r0 = _na_row_start(rq)
    if r0 <= rk < r0 + NA_WIN_R:
        return rk - rq + NA_WIN_R - 1
    return 2 * NA_WIN_R - 1


def _na_latent_kernel(q_ref, k_ref, v_ref, ck_ref, cv_ref, bias_ref, o_ref):
    ck = ck_ref[0, 0].astype(BF16)
    cv = _values_with_ones(cv_ref[0, 0].astype(BF16))
    v = _values_with_ones(v_ref[...])
    qrows = NA_QROWS * GRID_W
    lo = _lane_lo((GRID_W, LANES))
    work = []
    for qb in range(GRID_ROWS // NA_QROWS):
        w0 = min(max(NA_QROWS * qb - NA_WIN_R // 2, 0), GRID_ROWS - NA_WROWS)
        w0 -= w0 % 2
        win = slice(w0 * GRID_W, (w0 + NA_WROWS) * GRID_W)
        kwin = k_ref[win, :]
        values = tuple([v[half][win], cv[half]] for half in range(2))

        def bias_fn(half, qb=qb, w0=w0):
            rows = []
            for qi in range(NA_QROWS):
                rq = NA_QROWS * qb + qi
                blocks = []
                for kp in range(NA_WROWS // 2):
                    ia = _na_bias_index(rq, w0 + 2 * kp)
                    ib = _na_bias_index(rq, w0 + 2 * kp + 1)
                    blocks.append(jnp.where(lo, bias_ref[half, ia], bias_ref[half, ib]))
                rows.append(jnp.concatenate(blocks, axis=1))
            return jnp.concatenate(rows, axis=0)

        def store(o, qb=qb):
            o_ref[qb * qrows:(qb + 1) * qrows, :] = o.astype(BF16)

        work.append((lambda qb=qb: q_ref[qb * qrows:(qb + 1) * qrows, :], [kwin, ck], values, bias_fn, store))
    _pair_attention_pipeline(work)


def _na_latent(q, k, v, cache_k, cache_v, bias_tab, layer):
    tok = pl.BlockSpec((DEC_SEQ, LANES), lambda b, j: (b, j))
    cache = pl.BlockSpec((1, 1, PAST_LEN, LANES), lambda b, j: (b, layer, 0, j))
    return pl.pallas_call(
        _na_latent_kernel,
        grid=(DEC_BATCH, HEAD_PAIRS),
        in_specs=[tok, tok, tok, cache, cache,
                  pl.BlockSpec((None, 2, 2 * NA_WIN_R, GRID_W, LANES), lambda b, j: (layer, j, 0, 0, 0))],
        out_specs=tok,
        out_shape=jax.ShapeDtypeStruct(q.shape, BF16),
        compiler_params=_cparams(("arbitrary", "arbitrary")),
        name="na_latent",
    )(q, k, v, cache_k, cache_v, bias_tab)


def _gqa_latent_kernel(q_ref, k_ref, v_ref, ck_ref, cv_ref, o_ref):
    group = pl.program_id(1) // (HEAD_PAIRS // GQA_KV_HEADS)

    def both_halves(x):
        lane = lax.broadcasted_iota(jnp.int32, x.shape, 1)
        keep = jnp.where(lane < HEAD_DIM, 0, 1) == group
        return jnp.where(keep, x, pltpu.roll(x, HEAD_DIM, axis=1)).astype(BF16)

    ck = both_halves(ck_ref[0, 0])
    cv = _values_with_ones(both_halves(cv_ref[0, 0]))
    k = k_ref[...]
    v = _values_with_ones(v_ref[...])
    values = tuple([v[half], cv[half]] for half in range(2))
    work = []
    for qb in range(DEC_SEQ // GQA_QBLK):
        sl = slice(qb * GQA_QBLK, (qb + 1) * GQA_QBLK)

        def store(o, sl=sl):
            o_ref[sl, :] = o.astype(BF16)

        work.append((lambda sl=sl: q_ref[sl, :], [k, ck], values, None, store))
    _pair_attention_pipeline(work)


def _gqa_latent(q, k, v, cache_k, cache_v, layer):
    tok = pl.BlockSpec((DEC_SEQ, LANES), lambda b, j: (b, j))
    kv = pl.BlockSpec((DEC_SEQ, LANES), lambda b, j: (b, j // (HEAD_PAIRS // GQA_KV_HEADS)))
    cache = pl.BlockSpec((1, 1, PAST_LEN, GQA_KVW), lambda b, j: (b, layer, 0, 0))
    return pl.pallas_call(
        _gqa_latent_kernel,
        grid=(DEC_BATCH, HEAD_PAIRS),
        in_specs=[tok, kv, kv, cache, cache],
        out_specs=tok,
        out_shape=jax.ShapeDtypeStruct(q.shape, BF16),
        compiler_params=_cparams(("arbitrary", "arbitrary")),
        name="gqa_latent",
    )(q, k, v, cache_k, cache_v)


def _conv_kernel(u_ref, dw_ref, g_ref, b_ref, o_ref, pad_ref, *, L):
    zeros = jnp.zeros((CONV_PAD, CONV_C), F32)
    pad_ref[0:CONV_PAD, :] = zeros
    pad_ref[CONV_PAD + L:2 * CONV_PAD + L, :] = zeros
    pad_ref[CONV_PAD:CONV_PAD + L, :] = u_ref[...]
    first = CONV_PAD - CONV_W // 2

    def chunk(c, carry):
        base = pl.multiple_of(c * CONV_ROWS, CONV_ROWS)
        cols = []
        for lc in range(CONV_C // LANES):
            ls = slice(lc * LANES, (lc + 1) * LANES)
            acc = None
            for r in range(SUBLANES):
                z = None
                for k in range(CONV_W):
                    if (first + k) % SUBLANES != r:
                        continue
                    aligned = first + k - r
                    term = dw_ref[k:k + 1, ls] * pad_ref[pl.ds(base + aligned, CONV_ROWS + SUBLANES), ls]
                    z = term if z is None else z + term
                z = z[r:r + CONV_ROWS]
                acc = z if acc is None else acc + z
            cols.append(acc)
        acc = jnp.concatenate(cols, axis=1)
        mu = jnp.mean(acc, axis=-1, keepdims=True)
        d = acc - mu
        var = jnp.mean(d * d, axis=-1, keepdims=True)
        y = d * lax.rsqrt(var + EPS) * g_ref[...] + b_ref[...]
        o_ref[pl.ds(base, CONV_ROWS), :] = (y * jax.nn.sigmoid(y)).astype(BF16)
        return carry

    lax.fori_loop(0, L // CONV_ROWS, chunk, 0)


def _conv_branch(u, dw, ln_g, ln_b, *, L):
    T = u.shape[0]
    return pl.pallas_call(
        functools.partial(_conv_kernel, L=L),
        grid=(T // L,),
        in_specs=[pl.BlockSpec((L, CONV_C), lambda b: (b, 0)), _resident((CONV_W, CONV_C)),
                  _resident((1, CONV_C)), _resident((1, CONV_C))],
        out_specs=pl.BlockSpec((L, CONV_C), lambda b: (b, 0)),
        out_shape=jax.ShapeDtypeStruct((T, CONV_C), BF16),
        scratch_shapes=[pltpu.VMEM((L + 2 * CONV_PAD, CONV_C), F32)],
        compiler_params=_cparams(("arbitrary",)),
        name="conv_branch",
    )(u, dw, ln_g, ln_b)


def _sigmoid_tanh(x):
    return 0.5 * jnp.tanh(0.5 * x) + 0.5


def _lru_kernel(x_ref, cw_ref, cb_ref, wg_ref, bg_ref, lam_ref, h0_ref, y_ref, st_ref,
                pad_ref, a_ref, b_ref, *, L):
    zeros = jnp.zeros((LRU_PAD, LRU_W), F32)
    first = LRU_PAD - LRU_CONV_W // 2
    decay = LRU_C * jax.nn.log_sigmoid(lam_ref[...])
    for s in range(LRU_NB):
        pad_ref[s, 0:LRU_PAD, :] = zeros
        pad_ref[s, LRU_PAD + L:2 * LRU_PAD + L, :] = zeros
        pad_ref[s, LRU_PAD:LRU_PAD + L, :] = x_ref[s * L:(s + 1) * L, :]
        for c in range(L // LRU_GATE_ROWS):
            r0 = c * LRU_GATE_ROWS
            rows = slice(s * L + r0, s * L + r0 + LRU_GATE_ROWS)
            xc = cb_ref[...] + functools.reduce(jnp.add, [
                cw_ref[k:k + 1, :] * pad_ref[s, r0 + first + k:r0 + first + k + LRU_GATE_ROWS, :]
                for k in range(LRU_CONV_W)])
            xcb = xc.astype(BF16)

            def gate(d, which):
                z = jnp.concatenate([
                    jnp.dot(xcb[:, :LRU_HALF], wg_ref[d, which, 0], preferred_element_type=F32),
                    jnp.dot(xcb[:, LRU_HALF:], wg_ref[d, which, 1], preferred_element_type=F32)], axis=1)
                return _sigmoid_tanh(z + bg_ref[d, which:which + 1, :])

            for d in range(2):
                log_a = decay[d:d + 1, :] * gate(d, 0)
                a = jnp.exp(log_a)
                one_minus_a2 = -jnp.tanh(log_a) * (a * a + 1.0)
                a_ref[d, rows, :] = a
                b_ref[d, rows, :] = jnp.sqrt(one_minus_a2) * (gate(d, 1) * xc)

    def step(t, carry):
        out = []
        for s in range(LRU_NB):
            hf, hb = carry[2 * s], carry[2 * s + 1]
            tf = s * L + t
            tb = s * L + L - 1 - t
            hf = a_ref[0, pl.ds(tf, 1), :] * hf + b_ref[0, pl.ds(tf, 1), :]
            hb = a_ref[1, pl.ds(tb, 1), :] * hb + b_ref[1, pl.ds(tb, 1), :]
            b_ref[0, pl.ds(tf, 1), :] = hf
            b_ref[1, pl.ds(tb, 1), :] = hb
            out += [hf, hb]
        return tuple(out)

    init = []
    for s in range(LRU_NB):
        h0 = h0_ref[s]
        init += [h0[0:1, :], h0[1:2, :]]
    final = lax.fori_loop(0, L, step, tuple(init), unroll=8)
    y_ref[...] = (b_ref[0] + b_ref[1]).astype(BF16)
    for s in range(LRU_NB):
        st_ref[s] = jnp.concatenate([final[2 * s], final[2 * s + 1]], axis=0)


def _lru_branch(x, cw, cb, wg, bg, lam, h0, *, L):
    T = x.shape[0]
    nb = T // L
    rows = LRU_NB * L
    return pl.pallas_call(
        functools.partial(_lru_kernel, L=L),
        grid=(nb // LRU_NB,),
        in_specs=[pl.BlockSpec((rows, LRU_W), lambda b: (b, 0)), _resident((LRU_CONV_W, LRU_W)),
                  _resident((1, LRU_W)), _resident((2, 2, 2, LRU_HALF, LRU_HALF)),
                  _resident((2, 2, LRU_W)), _resident((2, LRU_W)),
                  pl.BlockSpec((LRU_NB, 2, LRU_W), lambda b: (b, 0, 0))],
        out_specs=[pl.BlockSpec((rows, LRU_W), lambda b: (b, 0)),
                   pl.BlockSpec((LRU_NB, 2, LRU_W), lambda b: (b, 0, 0))],
        out_shape=[jax.ShapeDtypeStruct((T, LRU_W), BF16), jax.ShapeDtypeStruct((nb, 2, LRU_W), F32)],
        scratch_shapes=[pltpu.VMEM((LRU_NB, L + 2 * LRU_PAD, LRU_W), F32), pltpu.VMEM((2, rows, LRU_W), F32),
                        pltpu.VMEM((2, rows, LRU_W), F32)],
        compiler_params=_cparams(("arbitrary",)),
        name="lru_branch",
    )(x, cw, cb, wg, bg, lam, h0)


def _merge_kernel(x_ref, mod_ref, g_ref, wg_ref, b0_ref, b1_ref, b2_ref, b3_ref, wb_ref, wo_ref, o_ref):
    x = x_ref[...]
    mod = mod_ref[0]
    h = _modulate(x, g_ref[2:3], mod, 1).astype(BF16)
    merged = None
    for k, br in enumerate((b0_ref, b1_ref, b2_ref, b3_ref)):
        gate_cols = slice(OFF_GATE + k * D_MODEL, OFF_GATE + (k + 1) * D_MODEL)
        logits = jnp.dot(h, wg_ref[:, gate_cols], preferred_element_type=F32)
        term = jax.nn.sigmoid(logits) * jnp.dot(br[...], wb_ref[k], preferred_element_type=F32)
        merged = term if merged is None else merged + term
    y = jnp.dot(merged.astype(BF16), wo_ref[...], preferred_element_type=F32)
    o_ref[...] = x + mod[5:6] * (_rms(y) * g_ref[3:4])


def _merge(x, mod, g, w_in, branches, w_branch, w_out, *, layer, tokens_per_mod):
    T = x.shape[0]
    tm = TM_MERGE
    row = lambda n: pl.BlockSpec((tm, n), lambda i: (i, 0))
    return pl.pallas_call(
        _merge_kernel,
        grid=(T // tm,),
        in_specs=[row(D_MODEL), pl.BlockSpec((1, 9, D_MODEL), lambda i: (i * tm // tokens_per_mod, 0, 0)),
                  _resident((6, D_MODEL)), _stacked((layer,), (D_MODEL, IN_COLS)),
                  row(BRANCH_W), row(BRANCH_W), row(BRANCH_W), row(BRANCH_W),
                  _stacked((layer,), (N_BRANCH, BRANCH_W, D_MODEL)), _stacked((layer,), (D_MODEL, D_MODEL))],
        out_specs=row(D_MODEL),
        out_shape=jax.ShapeDtypeStruct((T, D_MODEL), F32),
        compiler_params=_cparams(("arbitrary",)),
        name="merge",
    )(x, mod, g, w_in, *branches, w_branch, w_out)


def _rope_tables():
    t = np.arange(DEC_SEQ)
    row = (t // GRID_W).astype(np.float32)
    col = (t % GRID_W).astype(np.float32)
    half = HEAD_DIM // 2
    freqs = (np.float32(ROPE_THETA) ** (-np.arange(0, half, 2, dtype=np.float32) / np.float32(half))).astype(np.float32)
    ar = row[:, None] * freqs
    ac = col[:, None] * freqs
    cos_h = np.concatenate([np.cos(ar), np.cos(ar), np.cos(ac), np.cos(ac)], axis=1)
    sin_h = np.concatenate([-np.sin(ar), np.sin(ar), -np.sin(ac), np.sin(ac)], axis=1)
    return (jnp.asarray(np.tile(cos_h, (1, 2)), F32), jnp.asarray(np.tile(sin_h, (1, 2)), F32))


def _na_bias_tables(na_rpb):
    c = np.arange(GRID_W)
    c0 = np.clip(c - NA_WIN_C // 2, 0, GRID_W - NA_WIN_C)
    col_ok = (c[None, :] >= c0[:, None]) & (c[None, :] < c0[:, None] + NA_WIN_C)
    dc = np.clip(c[None, :] - c[:, None], -(NA_WIN_C - 1), NA_WIN_C - 1) + NA_WIN_C - 1
    tab = jnp.where(col_ok, na_rpb[..., dc].astype(F32) * LOG2E, NEG_INF)
    masked = jnp.full(tab.shape[:2] + (1, GRID_W, GRID_W), NEG_INF, F32)
    tab = jnp.concatenate([tab, masked], axis=2)
    return jnp.concatenate([tab, tab], axis=-1)


def _block_diag_ones(n):
    return jnp.asarray(np.kron(np.eye(n // HEAD_DIM), np.ones((HEAD_DIM, HEAD_DIM))), BF16)


def _lru_gate_weights(lru_wr, lru_wi):
    w = jnp.stack([lru_wr, lru_wi], axis=2)
    per_half = LRU_BLOCKS // 2
    bw = LRU_W // LRU_BLOCKS
    w = w.reshape(DEPTH, 2, 2, 2, per_half, bw, bw)
    eye = jnp.asarray(np.eye(per_half), F32)
    bd = jnp.einsum('...nij,nm->...nimj', w, eye).reshape(DEPTH, 2, 2, 2, LRU_HALF, LRU_HALF)
    return bd.astype(BF16)


def kernel(x_prompt, x_sample, c, cache_na_k, cache_na_v, cache_gqa_k, cache_gqa_v, state_lru, c_ctx, w_ada, b_ada, norm_g, ffn_w1, ffn_w2, w_in, na_rpb, conv_dw, conv_ln_g, conv_ln_b, gqa_q_norm, gqa_k_norm, lru_conv_w, lru_conv_b, lru_wr, lru_br, lru_wi, lru_bi, lru_lambda, w_branch, w_out):
    tp = BATCH * SEQ
    ts = DEC_BATCH * DEC_SEQ
    c_all = jnp.concatenate([c_ctx[None, :], c, jnp.zeros((ADA_ROWS - 1 - DEC_BATCH, D_MODEL), F32)], axis=0)
    mods = _adaln(c_all, w_ada, b_ada).reshape(DEPTH, ADA_ROWS, 9, D_MODEL)

    w1 = ffn_w1.astype(BF16)
    w2 = ffn_w2.astype(BF16)
    w_in16 = w_in.astype(BF16)
    wb = w_branch.astype(BF16)
    wo = w_out.astype(BF16)
    lru_wg = _lru_gate_weights(lru_wr, lru_wi)
    lru_bg = jnp.stack([lru_br, lru_bi], axis=2)
    bias_tab = _na_bias_tables(na_rpb)
    rope_tabs = _rope_tables()
    ones_q = _block_diag_ones(GQA_QW)
    ones_k = _block_diag_ones(GQA_KVW)
    cna_k = cache_na_k.reshape(DEC_BATCH, DEPTH, PAST_LEN, NA_W)
    cna_v = cache_na_v.reshape(DEC_BATCH, DEPTH, PAST_LEN, NA_W)
    cgq_k = cache_gqa_k.reshape(DEC_BATCH, DEPTH, PAST_LEN, GQA_KVW)
    cgq_v = cache_gqa_v.reshape(DEC_BATCH, DEPTH, PAST_LEN, GQA_KVW)
    h0_prompt = jnp.zeros((BATCH, 2, LRU_W), F32)

    xp = x_prompt.reshape(tp, D_MODEL)
    xs = x_sample.reshape(ts, D_MODEL)
    nk_l, nv_l, gk_l, gv_l, st_l = [], [], [], [], []
    for l in range(DEPTH):
        g = norm_g[l]
        mod_p = mods[l, 0:1]
        mod_s = mods[l, 1:1 + DEC_BATCH]
        qn = jnp.tile(gqa_q_norm[l], GQA_HEADS)[None, :]
        kn = jnp.tile(gqa_k_norm[l], GQA_KV_HEADS)[None, :]
        lru_args = (lru_conv_w[l], lru_conv_b[l][None, :], lru_wg[l], lru_bg[l], lru_lambda[l])
        conv_args = (conv_dw[l], conv_ln_g[l][None, :], conv_ln_b[l][None, :])

        xp = _ffn(xp, mod_p, g, w1, w2, layer=l, slot=0, tokens_per_mod=tp)
        naq, nak, nav, u, gq, gk, gv, lx = _proj(xp, mod_p, g, w_in16, qn, kn, ones_q, ones_k, None,
                                                  layer=l, latent=False, tokens_per_mod=tp)
        o_na, o_gq = _ctx_attn(naq, nak, nav, gq, gk, gv)
        o_conv = _conv_branch(u, *conv_args, L=SEQ)
        o_lru, st = _lru_branch(lx, *lru_args, h0_prompt, L=SEQ)
        xp = _merge(xp, mod_p, g, w_in16, (o_na, o_conv, o_gq, o_lru), wb, wo, layer=l, tokens_per_mod=tp)
        xp = _ffn(xp, mod_p, g, w1, w2, layer=l, slot=1, tokens_per_mod=tp)
        nk_l.append(nak.reshape(BATCH, SEQ, NA_HEADS, HEAD_DIM))
        nv_l.append(nav.reshape(BATCH, SEQ, NA_HEADS, HEAD_DIM))
        gk_l.append(gk.reshape(BATCH, SEQ, GQA_KV_HEADS, HEAD_DIM))
        gv_l.append(gv.reshape(BATCH, SEQ, GQA_KV_HEADS, HEAD_DIM))
        st_l.append(st)

        xs = _ffn(xs, mod_s, g, w1, w2, layer=l, slot=0, tokens_per_mod=DEC_SEQ)
        naq, nak, nav, u, gq, gk, gv, lx = _proj(xs, mod_s, g, w_in16, qn, kn, ones_q, ones_k, rope_tabs,
                                                  layer=l, latent=True, tokens_per_mod=DEC_SEQ)
        o_na = _na_latent(naq, nak, nav, cna_k, cna_v, bias_tab, l)
        o_gq = _gqa_latent(gq, gk, gv, cgq_k, cgq_v, l)
        o_conv = _conv_branch(u, *conv_args, L=DEC_SEQ)
        o_lru, _ = _lru_branch(lx, *lru_args, state_lru[:, l], L=DEC_SEQ)
        xs = _merge(xs, mod_s, g, w_in16, (o_na, o_conv, o_gq, o_lru), wb, wo, layer=l, tokens_per_mod=DEC_SEQ)
        xs = _ffn(xs, mod_s, g, w1, w2, layer=l, slot=1, tokens_per_mod=DEC_SEQ)

    return (xp.reshape(BATCH, SEQ, D_MODEL), xs.reshape(DEC_BATCH, DEC_SEQ, D_MODEL),
            jnp.stack(nk_l, axis=1), jnp.stack(nv_l, axis=1), jnp.stack(gk_l, axis=1), jnp.stack(gv_l, axis=1),
            jnp.stack(st_l, axis=1))
```

```python
import functools

import numpy as np
import jax
import jax.numpy as jnp
from jax import lax
from jax.experimental import pallas as pl
from jax.experimental.pallas import tpu as pltpu

D_MODEL = 1024
BATCH = 16
SEQ = 256
DEPTH = 4
DEC_BATCH = 8
DEC_SEQ = 1024
PAST_LEN = 512
GRID_W = 64
GRID_ROWS = DEC_SEQ // GRID_W
HEAD_DIM = 64
NA_HEADS = 8
NA_WIN_R = 8
NA_WIN_C = 16
CONV_C = 512
CONV_W = 31
GQA_HEADS = 8
GQA_KV_HEADS = 2
LRU_W = 512
LRU_BLOCKS = 8
LRU_CONV_W = 4
LRU_C = 8.0
N_BRANCH = 4
BRANCH_W = 512
D_FF = 2816
ROPE_THETA = 10000.0
EPS = 1e-6
NEG_INF = -1e30

NA_W = NA_HEADS * HEAD_DIM
GQA_QW = GQA_HEADS * HEAD_DIM
GQA_KVW = GQA_KV_HEADS * HEAD_DIM
OFF_NA = 0
OFF_CONV = OFF_NA + 3 * NA_W
OFF_GQA = OFF_CONV + 2 * CONV_C
OFF_LRU = OFF_GQA + GQA_QW + 2 * GQA_KVW
OFF_GATE = OFF_LRU + LRU_W
IN_COLS = OFF_GATE + N_BRANCH * D_MODEL
N_MOD = 9 * D_MODEL

F32 = jnp.float32
BF16 = jnp.bfloat16

LANES = 128
SUBLANES = 8
HEAD_PAIRS = NA_HEADS // 2
LOG2E = 1.4426950408889634
ATTN_SCALE = HEAD_DIM ** -0.5 * LOG2E
ADA_ROWS = 16
VMEM_LIMIT = 56 * 1024 * 1024

TM_FFN = 512
TM_PROJ = 256
TM_MERGE = 256
FF_CHUNK = 256
ADA_TN = 1152
CONV_ROWS = 128
CONV_PAD = 16
LRU_PAD = 8
LRU_GATE_ROWS = 256
LRU_HALF = LRU_W // 2
LRU_NB = 2
NA_QROWS = 2
NA_WROWS = 10
GQA_QBLK = 256
CTX_LAG = 2


def _cparams(sem):
    return pltpu.CompilerParams(dimension_semantics=sem, vmem_limit_bytes=VMEM_LIMIT)


def _rms(x):
    return x * lax.rsqrt(jnp.mean(x * x, axis=-1, keepdims=True) + EPS)


def _modulate(x, g_pre, mod, sub):
    shift = mod[3 * sub:3 * sub + 1]
    scale = mod[3 * sub + 1:3 * sub + 2]
    return (_rms(x) * g_pre) * (1.0 + scale) + shift


def _stacked(idx, shape):
    idx = tuple(idx)
    return pl.BlockSpec((None,) * len(idx) + tuple(shape), lambda *_: idx + (0,) * len(shape),
                        pipeline_mode=pl.Buffered(1))


def _resident(shape):
    return _stacked((), shape)


def _adaln_kernel(c_ref, w_ref, b_ref, o_ref):
    c = c_ref[...]
    s = (c * jax.nn.sigmoid(c)).astype(BF16)
    o_ref[0] = jnp.dot(s, w_ref[0].astype(BF16), preferred_element_type=F32) + b_ref[0]


def _adaln(c_all, w_ada, b_ada):
    return pl.pallas_call(
        _adaln_kernel,
        grid=(DEPTH, N_MOD // ADA_TN),
        in_specs=[
            pl.BlockSpec((ADA_ROWS, D_MODEL), lambda l, n: (0, 0)),
            pl.BlockSpec((1, D_MODEL, ADA_TN), lambda l, n: (l, 0, n)),
            pl.BlockSpec((1, 1, ADA_TN), lambda l, n: (l, 0, n)),
        ],
        out_specs=pl.BlockSpec((1, ADA_ROWS, ADA_TN), lambda l, n: (l, 0, n)),
        out_shape=jax.ShapeDtypeStruct((DEPTH, ADA_ROWS, N_MOD), F32),
        compiler_params=_cparams(("arbitrary", "arbitrary")),
        name="adaln",
    )(c_all, w_ada, b_ada.reshape(DEPTH, 1, N_MOD))


def _ffn_kernel(x_ref, mod_ref, g_ref, w1_ref, w2_ref, o_ref, acc_ref, *, sub, res_w):
    x = x_ref[...]
    mod = mod_ref[0]
    h = _modulate(x, g_ref[2 * sub:2 * sub + 1], mod, sub).astype(BF16)
    for c in range(D_FF // FF_CHUNK):
        lo = c * FF_CHUNK
        a = jnp.dot(h, w1_ref[:, lo:lo + FF_CHUNK].astype(BF16), preferred_element_type=F32)
        u = jnp.dot(h, w1_ref[:, D_FF + lo:D_FF + lo + FF_CHUNK].astype(BF16), preferred_element_type=F32)
        act = (a * jax.nn.sigmoid(a) * u).astype(BF16)
        part = jnp.dot(act, w2_ref[lo:lo + FF_CHUNK, :].astype(BF16), preferred_element_type=F32)
        if c == 0:
            acc_ref[...] = part
        else:
            acc_ref[...] += part
    gate = mod[3 * sub + 2:3 * sub + 3]
    o_ref[...] = x + (res_w * gate) * (_rms(acc_ref[...]) * g_ref[2 * sub + 1:2 * sub + 2])


def _ffn(x, mod, g, w1, w2, *, layer, slot, tokens_per_mod):
    T = x.shape[0]
    tm = TM_FFN
    return pl.pallas_call(
        functools.partial(_ffn_kernel, sub=2 * slot, res_w=0.5),
        grid=(T // tm,),
        in_specs=[
            pl.BlockSpec((tm, D_MODEL), lambda i: (i, 0)),
            pl.BlockSpec((1, 9, D_MODEL), lambda i: (i * tm // tokens_per_mod, 0, 0)),
            _resident((6, D_MODEL)),
            _stacked((layer, slot), (D_MODEL, 2 * D_FF)),
            _stacked((layer, slot), (D_FF, D_MODEL)),
        ],
        out_specs=pl.BlockSpec((tm, D_MODEL), lambda i: (i, 0)),
        out_shape=jax.ShapeDtypeStruct((T, D_MODEL), F32),
        scratch_shapes=[pltpu.VMEM((tm, D_MODEL), F32)],
        compiler_params=_cparams(("arbitrary",)),
        name="ffn",
    )(x, mod, g, w1, w2)


def _head_sumsq(x, ones_bd):
    x2 = x * x
    hi = x2.astype(BF16)
    lo = (x2 - hi.astype(F32)).astype(BF16)
    return jnp.dot(hi, ones_bd, preferred_element_type=F32) + jnp.dot(lo, ones_bd, preferred_element_type=F32)


def _head_rmsnorm(x, gain, ones_bd):
    ss = _head_sumsq(x, ones_bd)
    return x * lax.rsqrt(ss * (1.0 / HEAD_DIM) + EPS) * gain


def _rope(x, cos_t, sin_t):
    n = x.shape[-1]
    lane = lax.broadcasted_iota(jnp.int32, x.shape, 1)
    first = jnp.bitwise_and(lane, 31) < 16
    partner = jnp.where(first, pltpu.roll(x, n - 16, axis=1), pltpu.roll(x, 16, axis=1))
    return x * cos_t + partner * sin_t


def _dup_halves(x):
    lane = lax.broadcasted_iota(jnp.int32, x.shape, 1)
    swapped = pltpu.roll(x, HEAD_DIM, axis=1)
    lo = lane < HEAD_DIM
    return jnp.concatenate([jnp.where(lo, x, swapped), jnp.where(lo, swapped, x)], axis=1)


def _proj_kernel(*refs, latent):
    if latent:
        (x_ref, mod_ref, g_ref, w_ref, qn_ref, kn_ref, oq_ref, ok_ref, cos_ref, sin_ref,
         naq_ref, nak_ref, nav_ref, u_ref, gq_ref, gk_ref, gv_ref, lx_ref) = refs
    else:
        (x_ref, mod_ref, g_ref, w_ref, qn_ref, kn_ref, oq_ref, ok_ref,
         naq_ref, nak_ref, nav_ref, u_ref, gq_ref, gk_ref, gv_ref, lx_ref) = refs
    x = x_ref[...]
    h = _modulate(x, g_ref[2:3], mod_ref[0], 1).astype(BF16)

    def proj(lo, n):
        return jnp.dot(h, w_ref[:, lo:lo + n].astype(BF16), preferred_element_type=F32)

    naq_ref[...] = (proj(OFF_NA, NA_W) * ATTN_SCALE).astype(BF16)
    nak_ref[...] = proj(OFF_NA + NA_W, NA_W).astype(nak_ref.dtype)
    nav_ref[...] = proj(OFF_NA + 2 * NA_W, NA_W).astype(nav_ref.dtype)
    glu_a = proj(OFF_CONV, CONV_C)
    glu_g = proj(OFF_CONV + CONV_C, CONV_C)
    u_ref[...] = glu_a * jax.nn.sigmoid(glu_g)
    q = _head_rmsnorm(proj(OFF_GQA, GQA_QW), qn_ref[...], oq_ref[...])
    k = _head_rmsnorm(proj(OFF_GQA + GQA_QW, GQA_KVW), kn_ref[...], ok_ref[...])
    v = proj(OFF_GQA + GQA_QW + GQA_KVW, GQA_KVW)
    if latent:
        cos_t = cos_ref[...]
        sin_t = sin_ref[...]
        q = _rope(q, jnp.concatenate([cos_t] * (GQA_QW // LANES), axis=1),
                  jnp.concatenate([sin_t] * (GQA_QW // LANES), axis=1))
        k = _rope(k, cos_t, sin_t)
        gk_ref[...] = _dup_halves(k).astype(BF16)
        gv_ref[...] = _dup_halves(v).astype(BF16)
    else:
        gk_ref[...] = k
        gv_ref[...] = v
    gq_ref[...] = (q * ATTN_SCALE).astype(BF16)
    lx_ref[...] = proj(OFF_LRU, LRU_W)


def _proj(x, mod, g, w, qn, kn, ones_q, ones_k, rope_tabs, *, layer, latent, tokens_per_mod):
    T = x.shape[0]
    tm = TM_PROJ
    row = lambda n: pl.BlockSpec((tm, n), lambda i: (i, 0))
    in_specs = [
        row(D_MODEL),
        pl.BlockSpec((1, 9, D_MODEL), lambda i: (i * tm // tokens_per_mod, 0, 0)),
        _resident((6, D_MODEL)),
        _stacked((layer,), (D_MODEL, OFF_GATE)),
        _resident((1, GQA_QW)),
        _resident((1, GQA_KVW)),
        _resident((GQA_QW, GQA_QW)),
        _resident((GQA_KVW, GQA_KVW)),
    ]
    args = [x, mod, g, w, qn, kn, ones_q, ones_k]
    kv_dt = BF16 if latent else F32
    gkv_w = 2 * GQA_KVW if latent else GQA_KVW
    if latent:
        pos_blocks = DEC_SEQ // tm
        in_specs += [pl.BlockSpec((tm, LANES), lambda i: (i % pos_blocks, 0))] * 2
        args += list(rope_tabs)
    out_shape = [
        jax.ShapeDtypeStruct((T, NA_W), BF16),
        jax.ShapeDtypeStruct((T, NA_W), kv_dt),
        jax.ShapeDtypeStruct((T, NA_W), kv_dt),
        jax.ShapeDtypeStruct((T, CONV_C), F32),
        jax.ShapeDtypeStruct((T, GQA_QW), BF16),
        jax.ShapeDtypeStruct((T, gkv_w), kv_dt),
        jax.ShapeDtypeStruct((T, gkv_w), kv_dt),
        jax.ShapeDtypeStruct((T, LRU_W), F32),
    ]
    out_specs = [row(s.shape[1]) for s in out_shape]
    return pl.pallas_call(
        functools.partial(_proj_kernel, latent=latent),
        grid=(T // tm,),
        in_specs=in_specs,
        out_specs=out_specs,
        out_shape=out_shape,
        compiler_params=_cparams(("arbitrary",)),
        name="proj_latent" if latent else "proj_context",
    )(*args)


def _lane_lo(shape):
    return lax.broadcasted_iota(jnp.int32, shape, 1) < HEAD_DIM


def _qk(q, k):
    return lax.dot_general(q, k, (((1,), (1,)), ((), ())), preferred_element_type=F32)


def _values_with_ones(v):
    lo = _lane_lo(v.shape)
    ones = jnp.ones_like(v)
    return jnp.where(lo, v, ones), jnp.where(lo, ones, v)


def _pair_attention_pipeline(work, lag=1):
    heads = [(item, half) for item in work for half in range(2)]

    def scores_of(item, half):
        load_q, keys, _, bias_fn, _ = item
        q2 = load_q()
        lo = _lane_lo(q2.shape)
        qm = jnp.where(lo if half == 0 else jnp.logical_not(lo), q2, jnp.zeros_like(q2))
        scores = [_qk(qm, k) for k in keys]
        if bias_fn is not None:
            scores[0] = scores[0] + bias_fn(half)
        return scores

    def softmax_of(scores):
        m = functools.reduce(jnp.maximum, [jnp.max(s, axis=-1, keepdims=True) for s in scores])
        return [jnp.exp2(s - m).astype(BF16) for s in scores]

    def output_of(item, half, ps):
        return functools.reduce(jnp.add, [jnp.dot(p, v, preferred_element_type=F32)
                                          for p, v in zip(ps, item[2][half])])

    def finish(item, raw0, raw1):
        lo = _lane_lo(raw0.shape)
        denom = pltpu.roll(jnp.where(lo, raw1, raw0), HEAD_DIM, axis=1)
        item[4](jnp.where(lo, raw0, raw1) * (1.0 / denom))

    n = len(heads)
    scores, probs, raw = {}, {}, {}
    for t in range(n + 4 * lag):
        if t < n:
            scores[t] = scores_of(*heads[t])
        if 0 <= t - lag < n:
            probs[t - lag] = softmax_of(scores.pop(t - lag))
        if 0 <= t - 2 * lag < n:
            item, half = heads[t - 2 * lag]
            raw[t - 2 * lag] = output_of(item, half, probs.pop(t - 2 * lag))
        i = t - 2 * lag - 2
        if 0 <= i < n and i % 2 == 0:
            finish(heads[i][0], raw.pop(i), raw.pop(i + 1))


def _ctx_attn_kernel(naq_ref, nak_ref, nav_ref, gq_ref, gk_ref, gv_ref, ona_ref, ogq_ref):
    gk = _dup_halves(gk_ref[...]).astype(BF16)
    gv = _dup_halves(gv_ref[...]).astype(BF16)
    work = []
    for j in range(HEAD_PAIRS):
        sl = slice(j * LANES, (j + 1) * LANES)
        kvs = slice((j // 2) * LANES, (j // 2 + 1) * LANES)

        def store_na(o, sl=sl):
            ona_ref[:, sl] = o.astype(BF16)

        def store_gq(o, sl=sl):
            ogq_ref[:, sl] = o.astype(BF16)

        nav0, nav1 = _values_with_ones(nav_ref[:, sl].astype(BF16))
        gv0, gv1 = _values_with_ones(gv[:, kvs])
        work.append((lambda sl=sl: naq_ref[:, sl], [nak_ref[:, sl].astype(BF16)], ([nav0], [nav1]), None, store_na))
        work.append((lambda sl=sl: gq_ref[:, sl], [gk[:, kvs]], ([gv0], [gv1]), None, store_gq))
    _pair_attention_pipeline(work, lag=CTX_LAG)


def _ctx_attn(naq, nak, nav, gq, gk, gv):
    T = naq.shape[0]
    row = lambda n: pl.BlockSpec((SEQ, n), lambda b: (b, 0))
    return pl.pallas_call(
        _ctx_attn_kernel,
        grid=(T // SEQ,),
        in_specs=[row(NA_W), row(NA_W), row(NA_W), row(GQA_QW), row(GQA_KVW), row(GQA_KVW)],
        out_specs=[row(NA_W), row(GQA_QW)],
        out_shape=[jax.ShapeDtypeStruct((T, NA_W), BF16), jax.ShapeDtypeStruct((T, GQA_QW), BF16)],
        compiler_params=_cparams(("arbitrary",)),
        name="ctx_attn",
    )(naq, nak, nav, gq, gk, gv)


def _na_row_start(r):
    return min(max(r - NA_WIN_R // 2, 0), GRID_ROWS - NA_WIN_R)


def _na_bias_index(rq, rk):
    r0 = _na_row_start(rq)
    if r0 <= rk < r0 + NA_WIN_R:
        return rk - rq + NA_WIN_R - 1
    return None


def _na_latent_kernel(q_ref, k_ref, v_ref, ck_ref, cv_ref, bias_ref, o_ref):
    ck = ck_ref[0, 0].astype(BF16)
    cv = _values_with_ones(cv_ref[0, 0].astype(BF16))
    v = _values_with_ones(v_ref[...])
    qrows = NA_QROWS * GRID_W
    lo = _lane_lo((GRID_W, LANES))
    masked = jnp.full((GRID_W, LANES), NEG_INF, F32)
    work = []
    for qb in range(GRID_ROWS // NA_QROWS):
        w0 = min(max(NA_QROWS * qb - NA_WIN_R // 2, 0), GRID_ROWS - NA_WROWS)
        w0 -= w0 % 2
        win = slice(w0 * GRID_W, (w0 + NA_WROWS) * GRID_W)
        kwin = k_ref[win, :]
        values = tuple([v[half][win], cv[half]] for half in range(2))

        def bias_fn(half, qb=qb, w0=w0):
            rows = []
            for qi in range(NA_QROWS):
                rq = NA_QROWS * qb + qi
                blocks = []
                for kp in range(NA_WROWS // 2):
                    ia = _na_bias_index(rq, w0 + 2 * kp)
                    ib = _na_bias_index(rq, w0 + 2 * kp + 1)
                    blk_a = masked if ia is None else bias_ref[half, ia]
                    blk_b = masked if ib is None else bias_ref[half, ib]
                    blocks.append(masked if ia is None and ib is None else jnp.where(lo, blk_a, blk_b))
                rows.append(jnp.concatenate(blocks, axis=1))
            return jnp.concatenate(rows, axis=0)

        def store(o, qb=qb):
            o_ref[qb * qrows:(qb + 1) * qrows, :] = o.astype(BF16)

        work.append((lambda qb=qb: q_ref[qb * qrows:(qb + 1) * qrows, :], [kwin, ck], values, bias_fn, store))
    _pair_attention_pipeline(work)


def _na_latent(q, k, v, cache_k, cache_v, bias_tab, layer):
    tok = pl.BlockSpec((DEC_SEQ, LANES), lambda b, j: (b, j))
    cache = pl.BlockSpec((1, 1, PAST_LEN, LANES), lambda b, j: (b, layer, 0, j))
    return pl.pallas_call(
        _na_latent_kernel,
        grid=(DEC_BATCH, HEAD_PAIRS),
        in_specs=[tok, tok, tok, cache, cache,
                  pl.BlockSpec((None, 2, 2 * NA_WIN_R - 1, GRID_W, LANES), lambda b, j: (layer, j, 0, 0, 0))],
        out_specs=tok,
        out_shape=jax.ShapeDtypeStruct(q.shape, BF16),
        compiler_params=_cparams(("arbitrary", "arbitrary")),
        name="na_latent",
    )(q, k, v, cache_k, cache_v, bias_tab)


def _gqa_latent_kernel(q_ref, k_ref, v_ref, ck_ref, cv_ref, o_ref):
    group = pl.program_id(1) // (HEAD_PAIRS // GQA_KV_HEADS)

    def both_halves(x):
        lane = lax.broadcasted_iota(jnp.int32, x.shape, 1)
        keep = jnp.where(lane < HEAD_DIM, 0, 1) == group
        return jnp.where(keep, x, pltpu.roll(x, HEAD_DIM, axis=1)).astype(BF16)

    ck = both_halves(ck_ref[0, 0])
    cv = _values_with_ones(both_halves(cv_ref[0, 0]))
    k = k_ref[...]
    v = _values_with_ones(v_ref[...])
    values = tuple([v[half], cv[half]] for half in range(2))
    work = []
    for qb in range(DEC_SEQ // GQA_QBLK):
        sl = slice(qb * GQA_QBLK, (qb + 1) * GQA_QBLK)

        def store(o, sl=sl):
            o_ref[sl, :] = o.astype(BF16)

        work.append((lambda sl=sl: q_ref[sl, :], [k, ck], values, None, store))
    _pair_attention_pipeline(work)


def _gqa_latent(q, k, v, cache_k, cache_v, layer):
    tok = pl.BlockSpec((DEC_SEQ, LANES), lambda b, j: (b, j))
    kv = pl.BlockSpec((DEC_SEQ, LANES), lambda b, j: (b, j // (HEAD_PAIRS // GQA_KV_HEADS)))
    cache = pl.BlockSpec((1, 1, PAST_LEN, GQA_KVW), lambda b, j: (b, layer, 0, 0))
    return pl.pallas_call(
        _gqa_latent_kernel,
        grid=(DEC_BATCH, HEAD_PAIRS),
        in_specs=[tok, kv, kv, cache, cache],
        out_specs=tok,
        out_shape=jax.ShapeDtypeStruct(q.shape, BF16),
        compiler_params=_cparams(("arbitrary", "arbitrary")),
        name="gqa_latent",
    )(q, k, v, cache_k, cache_v)


def _conv_kernel(u_ref, dw_ref, g_ref, b_ref, o_ref, pad_ref, *, L):
    zeros = jnp.zeros((CONV_PAD, CONV_C), F32)
    pad_ref[0:CONV_PAD, :] = zeros
    pad_ref[CONV_PAD + L:2 * CONV_PAD + L, :] = zeros
    pad_ref[CONV_PAD:CONV_PAD + L, :] = u_ref[...]
    first = CONV_PAD - CONV_W // 2

    def chunk(c, carry):
        base = pl.multiple_of(c * CONV_ROWS, CONV_ROWS)
        cols = []
        for lc in range(CONV_C // LANES):
            ls = slice(lc * LANES, (lc + 1) * LANES)
            acc = None
            for r in range(SUBLANES):
                z = None
                for k in range(CONV_W):
                    if (first + k) % SUBLANES != r:
                        continue
                    aligned = first + k - r
                    term = dw_ref[k:k + 1, ls] * pad_ref[pl.ds(base + aligned, CONV_ROWS + SUBLANES), ls]
                    z = term if z is None else z + term
                z = z[r:r + CONV_ROWS]
                acc = z if acc is None else acc + z
            cols.append(acc)
        acc = jnp.concatenate(cols, axis=1)
        mu = jnp.mean(acc, axis=-1, keepdims=True)
        d = acc - mu
        var = jnp.mean(d * d, axis=-1, keepdims=True)
        y = d * lax.rsqrt(var + EPS) * g_ref[...] + b_ref[...]
        o_ref[pl.ds(base, CONV_ROWS), :] = (y * jax.nn.sigmoid(y)).astype(BF16)
        return carry

    lax.fori_loop(0, L // CONV_ROWS, chunk, 0)


def _conv_branch(u, dw, ln_g, ln_b, *, L):
    T = u.shape[0]
    return pl.pallas_call(
        functools.partial(_conv_kernel, L=L),
        grid=(T // L,),
        in_specs=[pl.BlockSpec((L, CONV_C), lambda b: (b, 0)), _resident((CONV_W, CONV_C)),
                  _resident((1, CONV_C)), _resident((1, CONV_C))],
        out_specs=pl.BlockSpec((L, CONV_C), lambda b: (b, 0)),
        out_shape=jax.ShapeDtypeStruct((T, CONV_C), BF16),
        scratch_shapes=[pltpu.VMEM((L + 2 * CONV_PAD, CONV_C), F32)],
        compiler_params=_cparams(("arbitrary",)),
        name="conv_branch",
    )(u, dw, ln_g, ln_b)


def _sigmoid_tanh(x):
    return 0.5 * jnp.tanh(0.5 * x) + 0.5


def _lru_kernel(x_ref, cw_ref, cb_ref, wg_ref, bg_ref, lam_ref, h0_ref, y_ref, st_ref,
                pad_ref, a_ref, b_ref, *, L):
    zeros = jnp.zeros((LRU_PAD, LRU_W), F32)
    first = LRU_PAD - LRU_CONV_W // 2
    decay = LRU_C * jax.nn.log_sigmoid(lam_ref[...])
    for s in range(LRU_NB):
        pad_ref[s, 0:LRU_PAD, :] = zeros
        pad_ref[s, LRU_PAD + L:2 * LRU_PAD + L, :] = zeros
        pad_ref[s, LRU_PAD:LRU_PAD + L, :] = x_ref[s * L:(s + 1) * L, :]
        for c in range(L // LRU_GATE_ROWS):
            r0 = c * LRU_GATE_ROWS
            rows = slice(s * L + r0, s * L + r0 + LRU_GATE_ROWS)
            xc = cb_ref[...] + functools.reduce(jnp.add, [
                cw_ref[k:k + 1, :] * pad_ref[s, r0 + first + k:r0 + first + k + LRU_GATE_ROWS, :]
                for k in range(LRU_CONV_W)])
            xcb = xc.astype(BF16)

            def gate(d, which):
                z = jnp.concatenate([
                    jnp.dot(xcb[:, :LRU_HALF], wg_ref[d, which, 0], preferred_element_type=F32),
                    jnp.dot(xcb[:, LRU_HALF:], wg_ref[d, which, 1], preferred_element_type=F32)], axis=1)
                return _sigmoid_tanh(z + bg_ref[d, which:which + 1, :])

            for d in range(2):
                log_a = decay[d:d + 1, :] * gate(d, 0)
                a = jnp.exp(log_a)
                one_minus_a2 = -jnp.tanh(log_a) * (a * a + 1.0)
                a_ref[d, rows, :] = a
                b_ref[d, rows, :] = jnp.sqrt(one_minus_a2) * (gate(d, 1) * xc)

    def block(i, carry):
        carry = list(carry)
        fwd = pl.multiple_of(i * SUBLANES, SUBLANES)
        bwd = pl.multiple_of(L - SUBLANES - i * SUBLANES, SUBLANES)
        for j in range(SUBLANES):
            for s in range(LRU_NB):
                hf, hb = carry[2 * s], carry[2 * s + 1]
                tf = pl.ds(s * L + fwd + j, 1)
                tb = pl.ds(s * L + bwd + (SUBLANES - 1 - j), 1)
                hf = a_ref[0, tf, :] * hf + b_ref[0, tf, :]
                hb = a_ref[1, tb, :] * hb + b_ref[1, tb, :]
                b_ref[0, tf, :] = hf
                b_ref[1, tb, :] = hb
                carry[2 * s], carry[2 * s + 1] = hf, hb
        return tuple(carry)

    init = []
    for s in range(LRU_NB):
        h0 = h0_ref[s]
        init += [h0[0:1, :], h0[1:2, :]]
    final = lax.fori_loop(0, L // SUBLANES, block, tuple(init))
    y_ref[...] = (b_ref[0] + b_ref[1]).astype(BF16)
    for s in range(LRU_NB):
        st_ref[s] = jnp.concatenate([final[2 * s], final[2 * s + 1]], axis=0)


def _lru_branch(x, cw, cb, wg, bg, lam, h0, *, L):
    T = x.shape[0]
    nb = T // L
    rows = LRU_NB * L
    return pl.pallas_call(
        functools.partial(_lru_kernel, L=L),
        grid=(nb // LRU_NB,),
        in_specs=[pl.BlockSpec((rows, LRU_W), lambda b: (b, 0)), _resident((LRU_CONV_W, LRU_W)),
                  _resident((1, LRU_W)), _resident((2, 2, 2, LRU_HALF, LRU_HALF)),
                  _resident((2, 2, LRU_W)), _resident((2, LRU_W)),
                  pl.BlockSpec((LRU_NB, 2, LRU_W), lambda b: (b, 0, 0))],
        out_specs=[pl.BlockSpec((rows, LRU_W), lambda b: (b, 0)),
                   pl.BlockSpec((LRU_NB, 2, LRU_W), lambda b: (b, 0, 0))],
        out_shape=[jax.ShapeDtypeStruct((T, LRU_W), BF16), jax.ShapeDtypeStruct((nb, 2, LRU_W), F32)],
        scratch_shapes=[pltpu.VMEM((LRU_NB, L + 2 * LRU_PAD, LRU_W), F32), pltpu.VMEM((2, rows, LRU_W), F32),
                        pltpu.VMEM((2, rows, LRU_W), F32)],
        compiler_params=_cparams(("arbitrary",)),
        name="lru_branch",
    )(x, cw, cb, wg, bg, lam, h0)


def _merge_kernel(x_ref, mod_ref, g_ref, wg_ref, b0_ref, b1_ref, b2_ref, b3_ref, wb_ref, wo_ref, o_ref):
    x = x_ref[...]
    mod = mod_ref[0]
    h = _modulate(x, g_ref[2:3], mod, 1).astype(BF16)
    merged = None
    for k, br in enumerate((b0_ref, b1_ref, b2_ref, b3_ref)):
        gate_cols = slice(OFF_GATE + k * D_MODEL, OFF_GATE + (k + 1) * D_MODEL)
        logits = jnp.dot(h, wg_ref[:, gate_cols].astype(BF16), preferred_element_type=F32)
        term = jax.nn.sigmoid(logits) * jnp.dot(br[...], wb_ref[k].astype(BF16), preferred_element_type=F32)
        merged = term if merged is None else merged + term
    y = jnp.dot(merged.astype(BF16), wo_ref[...].astype(BF16), preferred_element_type=F32)
    o_ref[...] = x + mod[5:6] * (_rms(y) * g_ref[3:4])


def _merge(x, mod, g, w_in, branches, w_branch, w_out, *, layer, tokens_per_mod):
    T = x.shape[0]
    tm = TM_MERGE
    row = lambda n: pl.BlockSpec((tm, n), lambda i: (i, 0))
    return pl.pallas_call(
        _merge_kernel,
        grid=(T // tm,),
        in_specs=[row(D_MODEL), pl.BlockSpec((1, 9, D_MODEL), lambda i: (i * tm // tokens_per_mod, 0, 0)),
                  _resident((6, D_MODEL)), _stacked((layer,), (D_MODEL, IN_COLS)),
                  row(BRANCH_W), row(BRANCH_W), row(BRANCH_W), row(BRANCH_W),
                  _stacked((layer,), (N_BRANCH, BRANCH_W, D_MODEL)), _stacked((layer,), (D_MODEL, D_MODEL))],
        out_specs=row(D_MODEL),
        out_shape=jax.ShapeDtypeStruct((T, D_MODEL), F32),
        compiler_params=_cparams(("arbitrary",)),
        name="merge",
    )(x, mod, g, w_in, *branches, w_branch, w_out)


def _rope_tables():
    t = np.arange(DEC_SEQ)
    row = (t // GRID_W).astype(np.float32)
    col = (t % GRID_W).astype(np.float32)
    half = HEAD_DIM // 2
    freqs = (np.float32(ROPE_THETA) ** (-np.arange(0, half, 2, dtype=np.float32) / np.float32(half))).astype(np.float32)
    ar = row[:, None] * freqs
    ac = col[:, None] * freqs
    cos_h = np.concatenate([np.cos(ar), np.cos(ar), np.cos(ac), np.cos(ac)], axis=1)
    sin_h = np.concatenate([-np.sin(ar), np.sin(ar), -np.sin(ac), np.sin(ac)], axis=1)
    return (jnp.asarray(np.tile(cos_h, (1, 2)), F32), jnp.asarray(np.tile(sin_h, (1, 2)), F32))


def _na_bias_tables(na_rpb):
    c = np.arange(GRID_W)
    c0 = np.clip(c - NA_WIN_C // 2, 0, GRID_W - NA_WIN_C)
    col_ok = (c[None, :] >= c0[:, None]) & (c[None, :] < c0[:, None] + NA_WIN_C)
    dc = np.clip(c[None, :] - c[:, None], -(NA_WIN_C - 1), NA_WIN_C - 1) + NA_WIN_C - 1
    col_ok = np.concatenate([col_ok, col_ok], axis=1)
    dc = np.concatenate([dc, dc], axis=1)
    return jnp.where(col_ok, na_rpb[..., dc].astype(F32) * LOG2E, NEG_INF)


def _block_diag_ones(n):
    return jnp.asarray(np.kron(np.eye(n // HEAD_DIM), np.ones((HEAD_DIM, HEAD_DIM))), BF16)


def _lru_gate_weights(lru_wr, lru_wi):
    w = jnp.stack([lru_wr, lru_wi], axis=2)
    per_half = LRU_BLOCKS // 2
    bw = LRU_W // LRU_BLOCKS
    w = w.reshape(DEPTH, 2, 2, 2, per_half, bw, bw)
    eye = jnp.asarray(np.eye(per_half), F32)
    bd = jnp.einsum('...nij,nm->...nimj', w, eye).reshape(DEPTH, 2, 2, 2, LRU_HALF, LRU_HALF)
    return bd.astype(BF16)


def kernel(x_prompt, x_sample, c, cache_na_k, cache_na_v, cache_gqa_k, cache_gqa_v, state_lru, c_ctx, w_ada, b_ada, norm_g, ffn_w1, ffn_w2, w_in, na_rpb, conv_dw, conv_ln_g, conv_ln_b, gqa_q_norm, gqa_k_norm, lru_conv_w, lru_conv_b, lru_wr, lru_br, lru_wi, lru_bi, lru_lambda, w_branch, w_out):
    tp = BATCH * SEQ
    ts = DEC_BATCH * DEC_SEQ
    c_all = jnp.concatenate([c_ctx[None, :], c, jnp.zeros((ADA_ROWS - 1 - DEC_BATCH, D_MODEL), F32)], axis=0)
    mods = _adaln(c_all, w_ada, b_ada).reshape(DEPTH, ADA_ROWS, 9, D_MODEL)

    w1, w2, w_in16, wb, wo = ffn_w1, ffn_w2, w_in, w_branch, w_out
    lru_wg = _lru_gate_weights(lru_wr, lru_wi)
    lru_bg = jnp.stack([lru_br, lru_bi], axis=2)
    bias_tab = _na_bias_tables(na_rpb)
    rope_tabs = _rope_tables()
    ones_q = _block_diag_ones(GQA_QW)
    ones_k = _block_diag_ones(GQA_KVW)
    cna_k = cache_na_k.reshape(DEC_BATCH, DEPTH, PAST_LEN, NA_W)
    cna_v = cache_na_v.reshape(DEC_BATCH, DEPTH, PAST_LEN, NA_W)
    cgq_k = cache_gqa_k.reshape(DEC_BATCH, DEPTH, PAST_LEN, GQA_KVW)
    cgq_v = cache_gqa_v.reshape(DEC_BATCH, DEPTH, PAST_LEN, GQA_KVW)
    h0_prompt = jnp.zeros((BATCH, 2, LRU_W), F32)

    xp = x_prompt.reshape(tp, D_MODEL)
    xs = x_sample.reshape(ts, D_MODEL)
    nk_l, nv_l, gk_l, gv_l, st_l = [], [], [], [], []
    for l in range(DEPTH):
        g = norm_g[l]
        mod_p = mods[l, 0:1]
        mod_s = mods[l, 1:1 + DEC_BATCH]
        qn = jnp.tile(gqa_q_norm[l], GQA_HEADS)[None, :]
        kn = jnp.tile(gqa_k_norm[l], GQA_KV_HEADS)[None, :]
        lru_args = (lru_conv_w[l], lru_conv_b[l][None, :], lru_wg[l], lru_bg[l], lru_lambda[l])
        conv_args = (conv_dw[l], conv_ln_g[l][None, :], conv_ln_b[l][None, :])

        xp = _ffn(xp, mod_p, g, w1, w2, layer=l, slot=0, tokens_per_mod=tp)
        naq, nak, nav, u, gq, gk, gv, lx = _proj(xp, mod_p, g, w_in16, qn, kn, ones_q, ones_k, None,
                                                  layer=l, latent=False, tokens_per_mod=tp)
        o_na, o_gq = _ctx_attn(naq, nak, nav, gq, gk, gv)
        o_conv = _conv_branch(u, *conv_args, L=SEQ)
        o_lru, st = _lru_branch(lx, *lru_args, h0_prompt, L=SEQ)
        xp = _merge(xp, mod_p, g, w_in16, (o_na, o_conv, o_gq, o_lru), wb, wo, layer=l, tokens_per_mod=tp)
        xp = _ffn(xp, mod_p, g, w1, w2, layer=l, slot=1, tokens_per_mod=tp)
        nk_l.append(nak.reshape(BATCH, SEQ, NA_HEADS, HEAD_DIM))
        nv_l.append(nav.reshape(BATCH, SEQ, NA_HEADS, HEAD_DIM))
        gk_l.append(gk.reshape(BATCH, SEQ, GQA_KV_HEADS, HEAD_DIM))
        gv_l.append(gv.reshape(BATCH, SEQ, GQA_KV_HEADS, HEAD_DIM))
        st_l.append(st)

        xs = _ffn(xs, mod_s, g, w1, w2, layer=l, slot=0, tokens_per_mod=DEC_SEQ)
        naq, nak, nav, u, gq, gk, gv, lx = _proj(xs, mod_s, g, w_in16, qn, kn, ones_q, ones_k, rope_tabs,
                                                  layer=l, latent=True, tokens_per_mod=DEC_SEQ)
        o_na = _na_latent(naq, nak, nav, cna_k, cna_v, bias_tab, l)
        o_gq = _gqa_latent(gq, gk, gv, cgq_k, cgq_v, l)
        o_conv = _conv_branch(u, *conv_args, L=DEC_SEQ)
        o_lru, _ = _lru_branch(lx, *lru_args, state_lru[:, l], L=DEC_SEQ)
        xs = _merge(xs, mod_s, g, w_in16, (o_na, o_conv, o_gq, o_lru), wb, wo, layer=l, tokens_per_mod=DEC_SEQ)
        xs = _ffn(xs, mod_s, g, w1, w2, layer=l, slot=1, tokens_per_mod=DEC_SEQ)

    return (xp.reshape(BATCH, SEQ, D_MODEL), xs.reshape(DEC_BATCH, DEC_SEQ, D_MODEL),
            jnp.stack(nk_l, axis=1), jnp.stack(nv_l, axis=1), jnp.stack(gk_l, axis=1), jnp.stack(gv_l, axis=1),
            jnp.stack(st_l, axis=1))
```

```python
import functools

import numpy as np
import jax
import jax.numpy as jnp
from jax import lax
from jax.experimental import pallas as pl
from jax.experimental.pallas import tpu as pltpu

D_MODEL = 1024
BATCH = 16
SEQ = 256
DEPTH = 4
DEC_BATCH = 8
DEC_SEQ = 1024
PAST_LEN = 512
GRID_W = 64
GRID_ROWS = DEC_SEQ // GRID_W
HEAD_DIM = 64
NA_HEADS = 8
NA_WIN_R = 8
NA_WIN_C = 16
CONV_C = 512
CONV_W = 31
GQA_HEADS = 8
GQA_KV_HEADS = 2
LRU_W = 512
LRU_BLOCKS = 8
LRU_CONV_W = 4
LRU_C = 8.0
N_BRANCH = 4
BRANCH_W = 512
D_FF = 2816
ROPE_THETA = 10000.0
EPS = 1e-6
NEG_INF = -1e30

NA_W = NA_HEADS * HEAD_DIM
GQA_QW = GQA_HEADS * HEAD_DIM
GQA_KVW = GQA_KV_HEADS * HEAD_DIM
OFF_NA = 0
OFF_CONV = OFF_NA + 3 * NA_W
OFF_GQA = OFF_CONV + 2 * CONV_C
OFF_LRU = OFF_GQA + GQA_QW + 2 * GQA_KVW
OFF_GATE = OFF_LRU + LRU_W
IN_COLS = OFF_GATE + N_BRANCH * D_MODEL
N_MOD = 9 * D_MODEL

F32 = jnp.float32
BF16 = jnp.bfloat16

LANES = 128
SUBLANES = 8
HEAD_PAIRS = NA_HEADS // 2
LOG2E = 1.4426950408889634
ATTN_SCALE = HEAD_DIM ** -0.5 * LOG2E
ADA_ROWS = 16
VMEM_LIMIT = 56 * 1024 * 1024

TM_FFN = 512
TM_PROJ = 256
TM_MERGE = 256
FF_CHUNK = 256
ADA_TN = 1152
CONV_ROWS = 128
CONV_PAD = 16
LRU_PAD = 8
LRU_GATE_ROWS = 256
LRU_HALF = LRU_W // 2
LRU_NB = 2
NA_QROWS = 2
NA_WROWS = 10
GQA_QBLK = 256


def _cparams(sem):
    return pltpu.CompilerParams(dimension_semantics=sem, vmem_limit_bytes=VMEM_LIMIT)


def _rms(x):
    return x * lax.rsqrt(jnp.mean(x * x, axis=-1, keepdims=True) + EPS)


def _modulate(x, g_pre, mod, sub):
    shift = mod[3 * sub:3 * sub + 1]
    scale = mod[3 * sub + 1:3 * sub + 2]
    return (_rms(x) * g_pre) * (1.0 + scale) + shift


def _stacked(idx, shape):
    idx = tuple(idx)
    return pl.BlockSpec((None,) * len(idx) + tuple(shape), lambda *_: idx + (0,) * len(shape),
                        pipeline_mode=pl.Buffered(1))


def _resident(shape):
    return _stacked((), shape)


def _adaln_kernel(c_ref, w_ref, b_ref, o_ref):
    c = c_ref[...]
    s = (c * jax.nn.sigmoid(c)).astype(BF16)
    o_ref[0] = jnp.dot(s, w_ref[0].astype(BF16), preferred_element_type=F32) + b_ref[0]


def _adaln(c_all, w_ada, b_ada):
    return pl.pallas_call(
        _adaln_kernel,
        grid=(DEPTH, N_MOD // ADA_TN),
        in_specs=[
            pl.BlockSpec((ADA_ROWS, D_MODEL), lambda l, n: (0, 0)),
            pl.BlockSpec((1, D_MODEL, ADA_TN), lambda l, n: (l, 0, n)),
            pl.BlockSpec((1, 1, ADA_TN), lambda l, n: (l, 0, n)),
        ],
        out_specs=pl.BlockSpec((1, ADA_ROWS, ADA_TN), lambda l, n: (l, 0, n)),
        out_shape=jax.ShapeDtypeStruct((DEPTH, ADA_ROWS, N_MOD), F32),
        compiler_params=_cparams(("arbitrary", "arbitrary")),
        name="adaln",
    )(c_all, w_ada, b_ada.reshape(DEPTH, 1, N_MOD))


def _ffn_tile(x_ref, mod_ref, g_ref, w1_ref, w2_ref, o_ref, acc_ref, *, sub, res_w):
    x = x_ref[...]
    mod = mod_ref[0]
    h = _modulate(x, g_ref[2 * sub:2 * sub + 1], mod, sub).astype(BF16)
    for c in range(D_FF // FF_CHUNK):
        lo = c * FF_CHUNK
        a = jnp.dot(h, w1_ref[:, lo:lo + FF_CHUNK].astype(BF16), preferred_element_type=F32)
        u = jnp.dot(h, w1_ref[:, D_FF + lo:D_FF + lo + FF_CHUNK].astype(BF16), preferred_element_type=F32)
        act = (a * jax.nn.sigmoid(a) * u).astype(BF16)
        part = jnp.dot(act, w2_ref[lo:lo + FF_CHUNK, :].astype(BF16), preferred_element_type=F32)
        if c == 0:
            acc_ref[...] = part
        else:
            acc_ref[...] += part
    gate = mod[3 * sub + 2:3 * sub + 3]
    o_ref[...] = x + (res_w * gate) * (_rms(acc_ref[...]) * g_ref[2 * sub + 1:2 * sub + 2])


def _ffn_kernel(xp_ref, xs_ref, mod_ref, g_ref, w1_ref, w2_ref, op_ref, os_ref, acc_ref, *, prompt_tiles, **kw):
    is_prompt = pl.program_id(0) < prompt_tiles

    @pl.when(is_prompt)
    def _():
        _ffn_tile(xp_ref, mod_ref, g_ref, w1_ref, w2_ref, op_ref, acc_ref, **kw)

    @pl.when(jnp.logical_not(is_prompt))
    def _():
        _ffn_tile(xs_ref, mod_ref, g_ref, w1_ref, w2_ref, os_ref, acc_ref, **kw)


def _ffn(xp, xs, mod, g, w1, w2, *, layer, slot):
    tm = TM_FFN
    nbp = xp.shape[0] // tm
    nbs = xs.shape[0] // tm
    prompt_tile = lambda i: (jnp.minimum(i, nbp - 1), 0)
    sample_tile = lambda i: (jnp.maximum(i - nbp, 0), 0)
    return pl.pallas_call(
        functools.partial(_ffn_kernel, sub=2 * slot, res_w=0.5, prompt_tiles=nbp),
        grid=(nbp + nbs,),
        in_specs=[
            pl.BlockSpec((tm, D_MODEL), prompt_tile),
            pl.BlockSpec((tm, D_MODEL), sample_tile),
            pl.BlockSpec((1, 9, D_MODEL), lambda i: (i * tm // DEC_SEQ, 0, 0)),
            _resident((6, D_MODEL)),
            _stacked((layer, slot), (D_MODEL, 2 * D_FF)),
            _stacked((layer, slot), (D_FF, D_MODEL)),
        ],
        out_specs=[pl.BlockSpec((tm, D_MODEL), prompt_tile), pl.BlockSpec((tm, D_MODEL), sample_tile)],
        out_shape=[jax.ShapeDtypeStruct(xp.shape, F32), jax.ShapeDtypeStruct(xs.shape, F32)],
        scratch_shapes=[pltpu.VMEM((tm, D_MODEL), F32)],
        compiler_params=_cparams(("arbitrary",)),
        name="ffn",
    )(xp, xs, mod, g, w1, w2)


def _head_sumsq(x, ones_bd):
    x2 = x * x
    hi = x2.astype(BF16)
    lo = (x2 - hi.astype(F32)).astype(BF16)
    return jnp.dot(hi, ones_bd, preferred_element_type=F32) + jnp.dot(lo, ones_bd, preferred_element_type=F32)


def _head_rmsnorm(x, gain, ones_bd):
    ss = _head_sumsq(x, ones_bd)
    return x * lax.rsqrt(ss * (1.0 / HEAD_DIM) + EPS) * gain


def _rope(x, cos_t, sin_t):
    n = x.shape[-1]
    lane = lax.broadcasted_iota(jnp.int32, x.shape, 1)
    first = jnp.bitwise_and(lane, 31) < 16
    partner = jnp.where(first, pltpu.roll(x, n - 16, axis=1), pltpu.roll(x, 16, axis=1))
    return x * cos_t + partner * sin_t


def _dup_halves(x):
    lane = lax.broadcasted_iota(jnp.int32, x.shape, 1)
    swapped = pltpu.roll(x, HEAD_DIM, axis=1)
    lo = lane < HEAD_DIM
    return jnp.concatenate([jnp.where(lo, x, swapped), jnp.where(lo, swapped, x)], axis=1)


def _proj_kernel(*refs, latent):
    if latent:
        (x_ref, mod_ref, g_ref, w_ref, qn_ref, kn_ref, oq_ref, ok_ref, cos_ref, sin_ref,
         naq_ref, nak_ref, nav_ref, u_ref, gq_ref, gk_ref, gv_ref, lx_ref) = refs
    else:
        (x_ref, mod_ref, g_ref, w_ref, qn_ref, kn_ref, oq_ref, ok_ref,
         naq_ref, nak_ref, nav_ref, u_ref, gq_ref, gk_ref, gv_ref, lx_ref) = refs
    x = x_ref[...]
    h = _modulate(x, g_ref[2:3], mod_ref[0], 1).astype(BF16)

    def proj(lo, n):
        return jnp.dot(h, w_ref[:, lo:lo + n].astype(BF16), preferred_element_type=F32)

    naq_ref[...] = (proj(OFF_NA, NA_W) * ATTN_SCALE).astype(BF16)
    nak_ref[...] = proj(OFF_NA + NA_W, NA_W).astype(nak_ref.dtype)
    nav_ref[...] = proj(OFF_NA + 2 * NA_W, NA_W).astype(nav_ref.dtype)
    glu_a = proj(OFF_CONV, CONV_C)
    glu_g = proj(OFF_CONV + CONV_C, CONV_C)
    u_ref[...] = glu_a * jax.nn.sigmoid(glu_g)
    q = _head_rmsnorm(proj(OFF_GQA, GQA_QW), qn_ref[...], oq_ref[...])
    k = _head_rmsnorm(proj(OFF_GQA + GQA_QW, GQA_KVW), kn_ref[...], ok_ref[...])
    v = proj(OFF_GQA + GQA_QW + GQA_KVW, GQA_KVW)
    if latent:
        cos_t = cos_ref[...]
        sin_t = sin_ref[...]
        q = _rope(q, jnp.concatenate([cos_t] * (GQA_QW // LANES), axis=1),
                  jnp.concatenate([sin_t] * (GQA_QW // LANES), axis=1))
        k = _rope(k, cos_t, sin_t)
        gk_ref[...] = _dup_halves(k).astype(BF16)
        gv_ref[...] = _dup_halves(v).astype(BF16)
    else:
        gk_ref[...] = k
        gv_ref[...] = v
    gq_ref[...] = (q * ATTN_SCALE).astype(BF16)
    lx_ref[...] = proj(OFF_LRU, LRU_W)


def _proj(x, mod, g, w, qn, kn, ones_q, ones_k, rope_tabs, *, layer, latent, tokens_per_mod):
    T = x.shape[0]
    tm = TM_PROJ
    row = lambda n: pl.BlockSpec((tm, n), lambda i: (i, 0))
    in_specs = [
        row(D_MODEL),
        pl.BlockSpec((1, 9, D_MODEL), lambda i: (i * tm // tokens_per_mod, 0, 0)),
        _resident((6, D_MODEL)),
        _stacked((layer,), (D_MODEL, OFF_GATE)),
        _resident((1, GQA_QW)),
        _resident((1, GQA_KVW)),
        _resident((GQA_QW, GQA_QW)),
        _resident((GQA_KVW, GQA_KVW)),
    ]
    args = [x, mod, g, w, qn, kn, ones_q, ones_k]
    kv_dt = BF16 if latent else F32
    gkv_w = 2 * GQA_KVW if latent else GQA_KVW
    if latent:
        pos_blocks = DEC_SEQ // tm
        in_specs += [pl.BlockSpec((tm, LANES), lambda i: (i % pos_blocks, 0))] * 2
        args += list(rope_tabs)
    out_shape = [
        jax.ShapeDtypeStruct((T, NA_W), BF16),
        jax.ShapeDtypeStruct((T, NA_W), kv_dt),
        jax.ShapeDtypeStruct((T, NA_W), kv_dt),
        jax.ShapeDtypeStruct((T, CONV_C), F32),
        jax.ShapeDtypeStruct((T, GQA_QW), BF16),
        jax.ShapeDtypeStruct((T, gkv_w), kv_dt),
        jax.ShapeDtypeStruct((T, gkv_w), kv_dt),
        jax.ShapeDtypeStruct((T, LRU_W), F32),
    ]
    out_specs = [row(s.shape[1]) for s in out_shape]
    return pl.pallas_call(
        functools.partial(_proj_kernel, latent=latent),
        grid=(T // tm,),
        in_specs=in_specs,
        out_specs=out_specs,
        out_shape=out_shape,
        compiler_params=_cparams(("arbitrary",)),
        name="proj_latent" if latent else "proj_context",
    )(*args)


def _lane_lo(shape):
    return lax.broadcasted_iota(jnp.int32, shape, 1) < HEAD_DIM


def _qk(q, k):
    return lax.dot_general(q, k, (((1,), (1,)), ((), ())), preferred_element_type=F32)


def _values_with_ones(v):
    lo = _lane_lo(v.shape)
    ones = jnp.ones_like(v)
    return jnp.where(lo, v, ones), jnp.where(lo, ones, v)


def _pair_attention_pipeline(work):
    heads = [(item, half) for item in work for half in range(2)]

    def scores_of(item, half):
        load_q, keys, _, bias_fn, _ = item
        q2 = load_q()
        lo = _lane_lo(q2.shape)
        qm = jnp.where(lo if half == 0 else jnp.logical_not(lo), q2, jnp.zeros_like(q2))
        scores = [_qk(qm, k) for k in keys]
        if bias_fn is not None:
            scores[0] = scores[0] + bias_fn(half)
        return scores

    def softmax_of(scores):
        m = functools.reduce(jnp.maximum, [jnp.max(s, axis=-1, keepdims=True) for s in scores])
        return [jnp.exp2(s - m).astype(BF16) for s in scores]

    def output_of(item, half, ps):
        return functools.reduce(jnp.add, [jnp.dot(p, v, preferred_element_type=F32)
                                          for p, v in zip(ps, item[2][half])])

    def finish(item, raw0, raw1):
        lo = _lane_lo(raw0.shape)
        denom = pltpu.roll(jnp.where(lo, raw1, raw0), HEAD_DIM, axis=1)
        item[4](jnp.where(lo, raw0, raw1) * (1.0 / denom))

    n = len(heads)
    scores, probs, raw = {}, {}, {}
    for t in range(n + 4):
        if t < n:
            scores[t] = scores_of(*heads[t])
        if 0 <= t - 1 < n:
            probs[t - 1] = softmax_of(scores.pop(t - 1))
        if 0 <= t - 2 < n:
            item, half = heads[t - 2]
            raw[t - 2] = output_of(item, half, probs.pop(t - 2))
        i = t - 4
        if 0 <= i < n and i % 2 == 0:
            finish(heads[i][0], raw.pop(i), raw.pop(i + 1))


def _pair_attention_small(q2, k2, v2):
    lo = _lane_lo(q2.shape)
    outs = []
    for half in range(2):
        qm = jnp.where(lo if half == 0 else jnp.logical_not(lo), q2, jnp.zeros_like(q2))
        s = _qk(qm, k2)
        p = jnp.exp2(s - jnp.max(s, axis=-1, keepdims=True))
        l = jnp.sum(p, axis=-1, keepdims=True)
        outs.append(jnp.dot(p.astype(BF16), v2, preferred_element_type=F32) * (1.0 / l))
    return jnp.where(lo, outs[0], outs[1])


def _ctx_attn_kernel(naq_ref, nak_ref, nav_ref, gq_ref, gk_ref, gv_ref, ona_ref, ogq_ref):
    gk = _dup_halves(gk_ref[...]).astype(BF16)
    gv = _dup_halves(gv_ref[...]).astype(BF16)
    for j in range(HEAD_PAIRS):
        sl = slice(j * LANES, (j + 1) * LANES)
        kvs = slice((j // 2) * LANES, (j // 2 + 1) * LANES)
        ona_ref[:, sl] = _pair_attention_small(
            naq_ref[:, sl], nak_ref[:, sl].astype(BF16), nav_ref[:, sl].astype(BF16)).astype(BF16)
        ogq_ref[:, sl] = _pair_attention_small(gq_ref[:, sl], gk[:, kvs], gv[:, kvs]).astype(BF16)


def _ctx_attn(naq, nak, nav, gq, gk, gv):
    T = naq.shape[0]
    row = lambda n: pl.BlockSpec((SEQ, n), lambda b: (b, 0))
    return pl.pallas_call(
        _ctx_attn_kernel,
        grid=(T // SEQ,),
        in_specs=[row(NA_W), row(NA_W), row(NA_W), row(GQA_QW), row(GQA_KVW), row(GQA_KVW)],
        out_specs=[row(NA_W), row(GQA_QW)],
        out_shape=[jax.ShapeDtypeStruct((T, NA_W), BF16), jax.ShapeDtypeStruct((T, GQA_QW), BF16)],
        compiler_params=_cparams(("arbitrary",)),
        name="ctx_attn",
    )(naq, nak, nav, gq, gk, gv)


def _na_row_start(r):
    return min(max(r - NA_WIN_R // 2, 0), GRID_ROWS - NA_WIN_R)


def _na_bias_index(rq, rk):
    r0 = _na_row_start(rq)
    if r0 <= rk < r0 + NA_WIN_R:
        return rk - rq + NA_WIN_R - 1
    return None


def _na_latent_kernel(q_ref, k_ref, v_ref, ck_ref, cv_ref, bias_ref, o_ref):
    ck = ck_ref[0, 0].astype(BF16)
    cv = _values_with_ones(cv_ref[0, 0].astype(BF16))
    v = _values_with_ones(v_ref[...])
    qrows = NA_QROWS * GRID_W
    lo = _lane_lo((GRID_W, LANES))
    masked = jnp.full((GRID_W, LANES), NEG_INF, F32)
    work = []
    for qb in range(GRID_ROWS // NA_QROWS):
        w0 = min(max(NA_QROWS * qb - NA_WIN_R // 2, 0), GRID_ROWS - NA_WROWS)
        w0 -= w0 % 2
        win = slice(w0 * GRID_W, (w0 + NA_WROWS) * GRID_W)
        kwin = k_ref[win, :]
        values = tuple([v[half][win], cv[half]] for half in range(2))

        def bias_fn(half, qb=qb, w0=w0):
            rows = []
            for qi in range(NA_QROWS):
                rq = NA_QROWS * qb + qi
                blocks = []
                for kp in range(NA_WROWS // 2):
                    ia = _na_bias_index(rq, w0 + 2 * kp)
                    ib = _na_bias_index(rq, w0 + 2 * kp + 1)
                    blk_a = masked if ia is None else bias_ref[half, ia]
                    blk_b = masked if ib is None else bias_ref[half, ib]
                    blocks.append(masked if ia is None and ib is None else jnp.where(lo, blk_a, blk_b))
                rows.append(jnp.concatenate(blocks, axis=1))
            return jnp.concatenate(rows, axis=0)

        def store(o, qb=qb):
            o_ref[qb * qrows:(qb + 1) * qrows, :] = o.astype(BF16)

        work.append((lambda qb=qb: q_ref[qb * qrows:(qb + 1) * qrows, :], [kwin, ck], values, bias_fn, store))
    _pair_attention_pipeline(work)


def _na_latent(q, k, v, cache_k, cache_v, bias_tab, layer):
    tok = pl.BlockSpec((DEC_SEQ, LANES), lambda b, j: (b, j))
    cache = pl.BlockSpec((1, 1, PAST_LEN, LANES), lambda b, j: (b, layer, 0, j))
    return pl.pallas_call(
        _na_latent_kernel,
        grid=(DEC_BATCH, HEAD_PAIRS),
        in_specs=[tok, tok, tok, cache, cache,
                  pl.BlockSpec((None, 2, 2 * NA_WIN_R - 1, GRID_W, LANES), lambda b, j: (layer, j, 0, 0, 0))],
        out_specs=tok,
        out_shape=jax.ShapeDtypeStruct(q.shape, BF16),
        compiler_params=_cparams(("arbitrary", "arbitrary")),
        name="na_latent",
    )(q, k, v, cache_k, cache_v, bias_tab)


def _gqa_latent_kernel(q_ref, k_ref, v_ref, ck_ref, cv_ref, o_ref):
    group = pl.program_id(1) // (HEAD_PAIRS // GQA_KV_HEADS)

    def both_halves(x):
        lane = lax.broadcasted_iota(jnp.int32, x.shape, 1)
        keep = jnp.where(lane < HEAD_DIM, 0, 1) == group
        return jnp.where(keep, x, pltpu.roll(x, HEAD_DIM, axis=1)).astype(BF16)

    ck = both_halves(ck_ref[0, 0])
    cv = _values_with_ones(both_halves(cv_ref[0, 0]))
    k = k_ref[...]
    v = _values_with_ones(v_ref[...])
    values = tuple([v[half], cv[half]] for half in range(2))
    work = []
    for qb in range(DEC_SEQ // GQA_QBLK):
        sl = slice(qb * GQA_QBLK, (qb + 1) * GQA_QBLK)

        def store(o, sl=sl):
            o_ref[sl, :] = o.astype(BF16)

        work.append((lambda sl=sl: q_ref[sl, :], [k, ck], values, None, store))
    _pair_attention_pipeline(work)


def _gqa_latent(q, k, v, cache_k, cache_v, layer):
    tok = pl.BlockSpec((DEC_SEQ, LANES), lambda b, j: (b, j))
    kv = pl.BlockSpec((DEC_SEQ, LANES), lambda b, j: (b, j // (HEAD_PAIRS // GQA_KV_HEADS)))
    cache = pl.BlockSpec((1, 1, PAST_LEN, GQA_KVW), lambda b, j: (b, layer, 0, 0))
    return pl.pallas_call(
        _gqa_latent_kernel,
        grid=(DEC_BATCH, HEAD_PAIRS),
        in_specs=[tok, kv, kv, cache, cache],
        out_specs=tok,
        out_shape=jax.ShapeDtypeStruct(q.shape, BF16),
        compiler_params=_cparams(("arbitrary", "arbitrary")),
        name="gqa_latent",
    )(q, k, v, cache_k, cache_v)


def _conv_kernel(u_ref, dw_ref, g_ref, b_ref, o_ref, pad_ref, *, L):
    zeros = jnp.zeros((CONV_PAD, CONV_C), F32)
    pad_ref[0:CONV_PAD, :] = zeros
    pad_ref[CONV_PAD + L:2 * CONV_PAD + L, :] = zeros
    pad_ref[CONV_PAD:CONV_PAD + L, :] = u_ref[...]
    first = CONV_PAD - CONV_W // 2

    def chunk(c, carry):
        base = pl.multiple_of(c * CONV_ROWS, CONV_ROWS)
        cols = []
        for lc in range(CONV_C // LANES):
            ls = slice(lc * LANES, (lc + 1) * LANES)
            acc = None
            for r in range(SUBLANES):
                z = None
                for k in range(CONV_W):
                    if (first + k) % SUBLANES != r:
                        continue
                    aligned = first + k - r
                    term = dw_ref[k:k + 1, ls] * pad_ref[pl.ds(base + aligned, CONV_ROWS + SUBLANES), ls]
                    z = term if z is None else z + term
                z = z[r:r + CONV_ROWS]
                acc = z if acc is None else acc + z
            cols.append(acc)
        acc = jnp.concatenate(cols, axis=1)
        mu = jnp.mean(acc, axis=-1, keepdims=True)
        d = acc - mu
        var = jnp.mean(d * d, axis=-1, keepdims=True)
        y = d * lax.rsqrt(var + EPS) * g_ref[...] + b_ref[...]
        o_ref[pl.ds(base, CONV_ROWS), :] = (y * jax.nn.sigmoid(y)).astype(BF16)
        return carry

    lax.fori_loop(0, L // CONV_ROWS, chunk, 0)


def _conv_branch(u, dw, ln_g, ln_b, *, L):
    T = u.shape[0]
    return pl.pallas_call(
        functools.partial(_conv_kernel, L=L),
        grid=(T // L,),
        in_specs=[pl.BlockSpec((L, CONV_C), lambda b: (b, 0)), _resident((CONV_W, CONV_C)),
                  _resident((1, CONV_C)), _resident((1, CONV_C))],
        out_specs=pl.BlockSpec((L, CONV_C), lambda b: (b, 0)),
        out_shape=jax.ShapeDtypeStruct((T, CONV_C), BF16),
        scratch_shapes=[pltpu.VMEM((L + 2 * CONV_PAD, CONV_C), F32)],
        compiler_params=_cparams(("arbitrary",)),
        name="conv_branch",
    )(u, dw, ln_g, ln_b)


def _sigmoid_tanh(x):
    return 0.5 * jnp.tanh(0.5 * x) + 0.5


def _lru_kernel(x_ref, cw_ref, cb_ref, wg_ref, bg_ref, lam_ref, h0_ref, y_ref, st_ref,
                pad_ref, a_ref, b_ref, *, L):
    zeros = jnp.zeros((LRU_PAD, LRU_W), F32)
    first = LRU_PAD - LRU_CONV_W // 2
    decay = LRU_C * jax.nn.log_sigmoid(lam_ref[...])
    for s in range(LRU_NB):
        pad_ref[s, 0:LRU_PAD, :] = zeros
        pad_ref[s, LRU_PAD + L:2 * LRU_PAD + L, :] = zeros
        pad_ref[s, LRU_PAD:LRU_PAD + L, :] = x_ref[s * L:(s + 1) * L, :]
        for c in range(L // LRU_GATE_ROWS):
            r0 = c * LRU_GATE_ROWS
            rows = slice(s * L + r0, s * L + r0 + LRU_GATE_ROWS)
            xc = cb_ref[...] + functools.reduce(jnp.add, [
                cw_ref[k:k + 1, :] * pad_ref[s, r0 + first + k:r0 + first + k + LRU_GATE_ROWS, :]
                for k in range(LRU_CONV_W)])
            xcb = xc.astype(BF16)

            def gate(d, which):
                z = jnp.concatenate([
                    jnp.dot(xcb[:, :LRU_HALF], wg_ref[d, which, 0], preferred_element_type=F32),
                    jnp.dot(xcb[:, LRU_HALF:], wg_ref[d, which, 1], preferred_element_type=F32)], axis=1)
                return _sigmoid_tanh(z + bg_ref[d, which:which + 1, :])

            for d in range(2):
                log_a = decay[d:d + 1, :] * gate(d, 0)
                a = jnp.exp(log_a)
                one_minus_a2 = -jnp.tanh(log_a) * (a * a + 1.0)
                a_ref[d, rows, :] = a
                b_ref[d, rows, :] = jnp.sqrt(one_minus_a2) * (gate(d, 1) * xc)

    def block(i, carry):
        carry = list(carry)
        fwd = pl.multiple_of(i * SUBLANES, SUBLANES)
        bwd = pl.multiple_of(L - SUBLANES - i * SUBLANES, SUBLANES)
        for j in range(SUBLANES):
            for s in range(LRU_NB):
                hf, hb = carry[2 * s], carry[2 * s + 1]
                tf = pl.ds(s * L + fwd + j, 1)
                tb = pl.ds(s * L + bwd + (SUBLANES - 1 - j), 1)
                hf = a_ref[0, tf, :] * hf + b_ref[0, tf, :]
                hb = a_ref[1, tb, :] * hb + b_ref[1, tb, :]
                b_ref[0, tf, :] = hf
                b_ref[1, tb, :] = hb
                carry[2 * s], carry[2 * s + 1] = hf, hb
        return tuple(carry)

    init = []
    for s in range(LRU_NB):
        h0 = h0_ref[s]
        init += [h0[0:1, :], h0[1:2, :]]
    final = lax.fori_loop(0, L // SUBLANES, block, tuple(init))
    y_ref[...] = (b_ref[0] + b_ref[1]).astype(BF16)
    for s in range(LRU_NB):
        st_ref[s] = jnp.concatenate([final[2 * s], final[2 * s + 1]], axis=0)


def _lru_branch(x, cw, cb, wg, bg, lam, h0, *, L):
    T = x.shape[0]
    nb = T // L
    rows = LRU_NB * L
    return pl.pallas_call(
        functools.partial(_lru_kernel, L=L),
        grid=(nb // LRU_NB,),
        in_specs=[pl.BlockSpec((rows, LRU_W), lambda b: (b, 0)), _resident((LRU_CONV_W, LRU_W)),
                  _resident((1, LRU_W)), _resident((2, 2, 2, LRU_HALF, LRU_HALF)),
                  _resident((2, 2, LRU_W)), _resident((2, LRU_W)),
                  pl.BlockSpec((LRU_NB, 2, LRU_W), lambda b: (b, 0, 0))],
        out_specs=[pl.BlockSpec((rows, LRU_W), lambda b: (b, 0)),
                   pl.BlockSpec((LRU_NB, 2, LRU_W), lambda b: (b, 0, 0))],
        out_shape=[jax.ShapeDtypeStruct((T, LRU_W), BF16), jax.ShapeDtypeStruct((nb, 2, LRU_W), F32)],
        scratch_shapes=[pltpu.VMEM((LRU_NB, L + 2 * LRU_PAD, LRU_W), F32), pltpu.VMEM((2, rows, LRU_W), F32),
                        pltpu.VMEM((2, rows, LRU_W), F32)],
        compiler_params=_cparams(("arbitrary",)),
        name="lru_branch",
    )(x, cw, cb, wg, bg, lam, h0)


def _merge_kernel(x_ref, mod_ref, g_ref, wg_ref, b0_ref, b1_ref, b2_ref, b3_ref, wb_ref, wo_ref, o_ref):
    x = x_ref[...]
    mod = mod_ref[0]
    h = _modulate(x, g_ref[2:3], mod, 1).astype(BF16)
    merged = None
    for k, br in enumerate((b0_ref, b1_ref, b2_ref, b3_ref)):
        gate_cols = slice(OFF_GATE + k * D_MODEL, OFF_GATE + (k + 1) * D_MODEL)
        logits = jnp.dot(h, wg_ref[:, gate_cols].astype(BF16), preferred_element_type=F32)
        term = jax.nn.sigmoid(logits) * jnp.dot(br[...], wb_ref[k].astype(BF16), preferred_element_type=F32)
        merged = term if merged is None else merged + term
    y = jnp.dot(merged.astype(BF16), wo_ref[...].astype(BF16), preferred_element_type=F32)
    o_ref[...] = x + mod[5:6] * (_rms(y) * g_ref[3:4])


def _merge(x, mod, g, w_in, branches, w_branch, w_out, *, layer, tokens_per_mod):
    T = x.shape[0]
    tm = TM_MERGE
    row = lambda n: pl.BlockSpec((tm, n), lambda i: (i, 0))
    return pl.pallas_call(
        _merge_kernel,
        grid=(T // tm,),
        in_specs=[row(D_MODEL), pl.BlockSpec((1, 9, D_MODEL), lambda i: (i * tm // tokens_per_mod, 0, 0)),
                  _resident((6, D_MODEL)), _stacked((layer,), (D_MODEL, IN_COLS)),
                  row(BRANCH_W), row(BRANCH_W), row(BRANCH_W), row(BRANCH_W),
                  _stacked((layer,), (N_BRANCH, BRANCH_W, D_MODEL)), _stacked((layer,), (D_MODEL, D_MODEL))],
        out_specs=row(D_MODEL),
        out_shape=jax.ShapeDtypeStruct((T, D_MODEL), F32),
        compiler_params=_cparams(("arbitrary",)),
        name="merge",
    )(x, mod, g, w_in, *branches, w_branch, w_out)


def _rope_tables():
    t = np.arange(DEC_SEQ)
    row = (t // GRID_W).astype(np.float32)
    col = (t % GRID_W).astype(np.float32)
    half = HEAD_DIM // 2
    freqs = (np.float32(ROPE_THETA) ** (-np.arange(0, half, 2, dtype=np.float32) / np.float32(half))).astype(np.float32)
    ar = row[:, None] * freqs
    ac = col[:, None] * freqs
    cos_h = np.concatenate([np.cos(ar), np.cos(ar), np.cos(ac), np.cos(ac)], axis=1)
    sin_h = np.concatenate([-np.sin(ar), np.sin(ar), -np.sin(ac), np.sin(ac)], axis=1)
    return (jnp.asarray(np.tile(cos_h, (1, 2)), F32), jnp.asarray(np.tile(sin_h, (1, 2)), F32))


def _na_bias_tables(na_rpb):
    c = np.arange(GRID_W)
    c0 = np.clip(c - NA_WIN_C // 2, 0, GRID_W - NA_WIN_C)
    col_ok = (c[None, :] >= c0[:, None]) & (c[None, :] < c0[:, None] + NA_WIN_C)
    dc = np.clip(c[None, :] - c[:, None], -(NA_WIN_C - 1), NA_WIN_C - 1) + NA_WIN_C - 1
    col_ok = np.concatenate([col_ok, col_ok], axis=1)
    dc = np.concatenate([dc, dc], axis=1)
    onehot = (np.arange(2 * NA_WIN_C - 1)[:, None, None] == dc[None]) & col_ok[None]
    tab = jnp.einsum('lhrm,mqk->lhrqk', na_rpb.astype(F32) * LOG2E, jnp.asarray(onehot, F32),
                     precision=lax.Precision.HIGHEST)
    return tab + jnp.asarray(np.where(col_ok, 0.0, NEG_INF), F32)


def _block_diag_ones(n):
    return jnp.asarray(np.kron(np.eye(n // HEAD_DIM), np.ones((HEAD_DIM, HEAD_DIM))), BF16)


def _lru_gate_weights(lru_wr, lru_wi):
    w = jnp.stack([lru_wr, lru_wi], axis=2)
    per_half = LRU_BLOCKS // 2
    bw = LRU_W // LRU_BLOCKS
    w = w.reshape(DEPTH, 2, 2, 2, per_half, bw, bw)
    eye = jnp.asarray(np.eye(per_half), F32)
    bd = jnp.einsum('...nij,nm->...nimj', w, eye).reshape(DEPTH, 2, 2, 2, LRU_HALF, LRU_HALF)
    return bd.astype(BF16)


def kernel(x_prompt, x_sample, c, cache_na_k, cache_na_v, cache_gqa_k, cache_gqa_v, state_lru, c_ctx, w_ada, b_ada, norm_g, ffn_w1, ffn_w2, w_in, na_rpb, conv_dw, conv_ln_g, conv_ln_b, gqa_q_norm, gqa_k_norm, lru_conv_w, lru_conv_b, lru_wr, lru_br, lru_wi, lru_bi, lru_lambda, w_branch, w_out):
    tp = BATCH * SEQ
    ts = DEC_BATCH * DEC_SEQ
    c_all = jnp.concatenate([c_ctx[None, :], c, jnp.zeros((ADA_ROWS - 1 - DEC_BATCH, D_MODEL), F32)], axis=0)
    mods = _adaln(c_all, w_ada, b_ada).reshape(DEPTH, ADA_ROWS, 9, D_MODEL)

    w1, w2, w_in16, wb, wo = ffn_w1, ffn_w2, w_in, w_branch, w_out
    lru_wg = _lru_gate_weights(lru_wr, lru_wi)
    lru_bg = jnp.stack([lru_br, lru_bi], axis=2)
    bias_tab = _na_bias_tables(na_rpb)
    rope_tabs = _rope_tables()
    ones_q = _block_diag_ones(GQA_QW)
    ones_k = _block_diag_ones(GQA_KVW)
    cna_k = cache_na_k.reshape(DEC_BATCH, DEPTH, PAST_LEN, NA_W)
    cna_v = cache_na_v.reshape(DEC_BATCH, DEPTH, PAST_LEN, NA_W)
    cgq_k = cache_gqa_k.reshape(DEC_BATCH, DEPTH, PAST_LEN, GQA_KVW)
    cgq_v = cache_gqa_v.reshape(DEC_BATCH, DEPTH, PAST_LEN, GQA_KVW)
    h0_prompt = jnp.zeros((BATCH, 2, LRU_W), F32)

    xp = x_prompt.reshape(tp, D_MODEL)
    xs = x_sample.reshape(ts, D_MODEL)
    nk_l, nv_l, gk_l, gv_l, st_l = [], [], [], [], []
    for l in range(DEPTH):
        g = norm_g[l]
        mod_p = mods[l, 0:1]
        mod_s = mods[l, 1:1 + DEC_BATCH]
        qn = jnp.tile(gqa_q_norm[l], GQA_HEADS)[None, :]
        kn = jnp.tile(gqa_k_norm[l], GQA_KV_HEADS)[None, :]
        lru_args = (lru_conv_w[l], lru_conv_b[l][None, :], lru_wg[l], lru_bg[l], lru_lambda[l])
        conv_args = (conv_dw[l], conv_ln_g[l][None, :], conv_ln_b[l][None, :])

        mod_all = jnp.concatenate([jnp.broadcast_to(mod_p, (tp // DEC_SEQ, 9, D_MODEL)), mod_s], axis=0)

        xp, xs = _ffn(xp, xs, mod_all, g, w1, w2, layer=l, slot=0)

        naq, nak, nav, u, gq, gk, gv, lx = _proj(xp, mod_p, g, w_in16, qn, kn, ones_q, ones_k, None,
                                                  layer=l, latent=False, tokens_per_mod=tp)
        o_na, o_gq = _ctx_attn(naq, nak, nav, gq, gk, gv)
        o_conv = _conv_branch(u, *conv_args, L=SEQ)
        o_lru, st = _lru_branch(lx, *lru_args, h0_prompt, L=SEQ)
        xp = _merge(xp, mod_p, g, w_in16, (o_na, o_conv, o_gq, o_lru), wb, wo, layer=l, tokens_per_mod=tp)
        nk_l.append(nak.reshape(BATCH, SEQ, NA_HEADS, HEAD_DIM))
        nv_l.append(nav.reshape(BATCH, SEQ, NA_HEADS, HEAD_DIM))
        gk_l.append(gk.reshape(BATCH, SEQ, GQA_KV_HEADS, HEAD_DIM))
        gv_l.append(gv.reshape(BATCH, SEQ, GQA_KV_HEADS, HEAD_DIM))
        st_l.append(st)

        naq, nak, nav, u, gq, gk, gv, lx = _proj(xs, mod_s, g, w_in16, qn, kn, ones_q, ones_k, rope_tabs,
                                                  layer=l, latent=True, tokens_per_mod=DEC_SEQ)
        o_na = _na_latent(naq, nak, nav, cna_k, cna_v, bias_tab, l)
        o_gq = _gqa_latent(gq, gk, gv, cgq_k, cgq_v, l)
        o_conv = _conv_branch(u, *conv_args, L=DEC_SEQ)
        o_lru, _ = _lru_branch(lx, *lru_args, state_lru[:, l], L=DEC_SEQ)
        xs = _merge(xs, mod_s, g, w_in16, (o_na, o_conv, o_gq, o_lru), wb, wo, layer=l, tokens_per_mod=DEC_SEQ)

        xp, xs = _ffn(xp, xs, mod_all, g, w1, w2, layer=l, slot=1)

    return (xp.reshape(BATCH, SEQ, D_MODEL), xs.reshape(DEC_BATCH, DEC_SEQ, D_MODEL),
            jnp.stack(nk_l, axis=1), jnp.stack(nv_l, axis=1), jnp.stack(gk_l, axis=1), jnp.stack(gv_l, axis=1),
            jnp.stack(st_l, axis=1))
```

```python
import functools

import numpy as np
import jax
import jax.numpy as jnp
from jax import lax
from jax.experimental import pallas as pl
from jax.experimental.pallas import tpu as pltpu

D_MODEL = 1024
BATCH = 16
SEQ = 256
DEPTH = 4
DEC_BATCH = 8
DEC_SEQ = 1024
PAST_LEN = 512
GRID_W = 64
GRID_ROWS = DEC_SEQ // GRID_W
HEAD_DIM = 64
NA_HEADS = 8
NA_WIN_R = 8
NA_WIN_C = 16
CONV_C = 512
CONV_W = 31
GQA_HEADS = 8
GQA_KV_HEADS = 2
LRU_W = 512
LRU_BLOCKS = 8
LRU_CONV_W = 4
LRU_C = 8.0
N_BRANCH = 4
BRANCH_W = 512
D_FF = 2816
ROPE_THETA = 10000.0
EPS = 1e-6
NEG_INF = -1e30

NA_W = NA_HEADS * HEAD_DIM
GQA_QW = GQA_HEADS * HEAD_DIM
GQA_KVW = GQA_KV_HEADS * HEAD_DIM
OFF_NA = 0
OFF_CONV = OFF_NA + 3 * NA_W
OFF_GQA = OFF_CONV + 2 * CONV_C
OFF_LRU = OFF_GQA + GQA_QW + 2 * GQA_KVW
OFF_GATE = OFF_LRU + LRU_W
IN_COLS = OFF_GATE + N_BRANCH * D_MODEL
N_MOD = 9 * D_MODEL
GATE_A = OFF_GATE
GATE_B = IN_COLS - OFF_GATE - GATE_A
assert GATE_B > 0 and OFF_GATE % GATE_A == 0 and (OFF_GATE + GATE_A) % GATE_B == 0 and GATE_B % 128 == 0

F32 = jnp.float32
BF16 = jnp.bfloat16

LANES = 128
SUBLANES = 8
HEAD_PAIRS = NA_HEADS // 2
LOG2E = 1.4426950408889634
ATTN_SCALE = HEAD_DIM ** -0.5 * LOG2E
ADA_ROWS = 16
VMEM_LIMIT = 56 * 1024 * 1024

TM_FFN = 512
TM_PROJ = 256
TM_MERGE = 256
FF_CHUNK = 256
ADA_TN = 1152
CONV_ROWS = 128
CONV_PAD = 16
LRU_PAD = 8
LRU_GATE_ROWS = 256
LRU_HALF = LRU_W // 2
LRU_NB = 2
NA_QROWS = 2
NA_WROWS = 10
GQA_QBLK = 256


def _cparams(sem):
    return pltpu.CompilerParams(dimension_semantics=sem, vmem_limit_bytes=VMEM_LIMIT)


def _rms(x):
    return x * lax.rsqrt(jnp.mean(x * x, axis=-1, keepdims=True) + EPS)


def _modulate(x, g_pre, mod, sub):
    shift = mod[3 * sub:3 * sub + 1]
    scale = mod[3 * sub + 1:3 * sub + 2]
    return (_rms(x) * g_pre) * (1.0 + scale) + shift


def _stacked(idx, shape):
    idx = tuple(idx)
    return pl.BlockSpec((None,) * len(idx) + tuple(shape), lambda *_: idx + (0,) * len(shape),
                        pipeline_mode=pl.Buffered(1))


def _resident(shape):
    return _stacked((), shape)


def _adaln_kernel(c_ref, w_ref, b_ref, o_ref):
    c = c_ref[...]
    s = (c * jax.nn.sigmoid(c)).astype(BF16)
    o_ref[0] = jnp.dot(s, w_ref[0].astype(BF16), preferred_element_type=F32) + b_ref[0]


def _adaln(c_all, w_ada, b_ada):
    return pl.pallas_call(
        _adaln_kernel,
        grid=(DEPTH, N_MOD // ADA_TN),
        in_specs=[
            pl.BlockSpec((ADA_ROWS, D_MODEL), lambda l, n: (0, 0)),
            pl.BlockSpec((1, D_MODEL, ADA_TN), lambda l, n: (l, 0, n)),
            pl.BlockSpec((1, 1, ADA_TN), lambda l, n: (l, 0, n)),
        ],
        out_specs=pl.BlockSpec((1, ADA_ROWS, ADA_TN), lambda l, n: (l, 0, n)),
        out_shape=jax.ShapeDtypeStruct((DEPTH, ADA_ROWS, N_MOD), F32),
        compiler_params=_cparams(("arbitrary", "arbitrary")),
        name="adaln",
    )(c_all, w_ada, b_ada.reshape(DEPTH, 1, N_MOD))


def _ffn_tile(x_ref, mod_ref, g_ref, w1_ref, w2_ref, o_ref, acc_ref, *, sub, res_w):
    x = x_ref[...]
    mod = mod_ref[0]
    h = _modulate(x, g_ref[2 * sub:2 * sub + 1], mod, sub).astype(BF16)
    for c in range(D_FF // FF_CHUNK):
        lo = c * FF_CHUNK
        a = jnp.dot(h, w1_ref[:, lo:lo + FF_CHUNK].astype(BF16), preferred_element_type=F32)
        u = jnp.dot(h, w1_ref[:, D_FF + lo:D_FF + lo + FF_CHUNK].astype(BF16), preferred_element_type=F32)
        act = (a * jax.nn.sigmoid(a) * u).astype(BF16)
        part = jnp.dot(act, w2_ref[lo:lo + FF_CHUNK, :].astype(BF16), preferred_element_type=F32)
        if c == 0:
            acc_ref[...] = part
        else:
            acc_ref[...] += part
    gate = mod[3 * sub + 2:3 * sub + 3]
    o_ref[...] = x + (res_w * gate) * (_rms(acc_ref[...]) * g_ref[2 * sub + 1:2 * sub + 2])


def _ffn_kernel(xp_ref, xs_ref, mod_ref, g_ref, w1_ref, w2_ref, op_ref, os_ref, acc_ref, *, prompt_tiles, **kw):
    is_prompt = pl.program_id(0) < prompt_tiles

    @pl.when(is_prompt)
    def _():
        _ffn_tile(xp_ref, mod_ref, g_ref, w1_ref, w2_ref, op_ref, acc_ref, **kw)

    @pl.when(jnp.logical_not(is_prompt))
    def _():
        _ffn_tile(xs_ref, mod_ref, g_ref, w1_ref, w2_ref, os_ref, acc_ref, **kw)


def _ffn(xp, xs, mod, g, w1, w2, *, layer, slot):
    tm = TM_FFN
    nbp = xp.shape[0] // tm
    nbs = xs.shape[0] // tm
    prompt_tile = lambda i: (jnp.minimum(i, nbp - 1), 0)
    sample_tile = lambda i: (jnp.maximum(i - nbp, 0), 0)
    return pl.pallas_call(
        functools.partial(_ffn_kernel, sub=2 * slot, res_w=0.5, prompt_tiles=nbp),
        grid=(nbp + nbs,),
        in_specs=[
            pl.BlockSpec((tm, D_MODEL), prompt_tile),
            pl.BlockSpec((tm, D_MODEL), sample_tile),
            pl.BlockSpec((1, 9, D_MODEL), lambda i: (i * tm // DEC_SEQ, 0, 0)),
            _resident((6, D_MODEL)),
            _stacked((layer, slot), (D_MODEL, 2 * D_FF)),
            _stacked((layer, slot), (D_FF, D_MODEL)),
        ],
        out_specs=[pl.BlockSpec((tm, D_MODEL), prompt_tile), pl.BlockSpec((tm, D_MODEL), sample_tile)],
        out_shape=[jax.ShapeDtypeStruct(xp.shape, F32), jax.ShapeDtypeStruct(xs.shape, F32)],
        scratch_shapes=[pltpu.VMEM((tm, D_MODEL), F32)],
        compiler_params=_cparams(("arbitrary",)),
        name="ffn",
    )(xp, xs, mod, g, w1, w2)


def _head_sumsq(x, ones_bd):
    x2 = x * x
    hi = x2.astype(BF16)
    lo = (x2 - hi.astype(F32)).astype(BF16)
    return jnp.dot(hi, ones_bd, preferred_element_type=F32) + jnp.dot(lo, ones_bd, preferred_element_type=F32)


def _head_rmsnorm(x, gain, ones_bd):
    ss = _head_sumsq(x, ones_bd)
    return x * lax.rsqrt(ss * (1.0 / HEAD_DIM) + EPS) * gain


def _rope(x, cos_t, sin_t):
    n = x.shape[-1]
    lane = lax.broadcasted_iota(jnp.int32, x.shape, 1)
    first = jnp.bitwise_and(lane, 31) < 16
    partner = jnp.where(first, pltpu.roll(x, n - 16, axis=1), pltpu.roll(x, 16, axis=1))
    return x * cos_t + partner * sin_t


def _dup_halves(x):
    lane = lax.broadcasted_iota(jnp.int32, x.shape, 1)
    swapped = pltpu.roll(x, HEAD_DIM, axis=1)
    lo = lane < HEAD_DIM
    return jnp.concatenate([jnp.where(lo, x, swapped), jnp.where(lo, swapped, x)], axis=1)


def _proj_kernel(*refs, latent):
    if latent:
        (x_ref, mod_ref, g_ref, w_ref, qn_ref, kn_ref, oq_ref, ok_ref, cos_ref, sin_ref,
         naq_ref, nak_ref, nav_ref, u_ref, gq_ref, gk_ref, gv_ref, lx_ref) = refs
    else:
        (x_ref, mod_ref, g_ref, w_ref, qn_ref, kn_ref, oq_ref, ok_ref,
         naq_ref, nak_ref, nav_ref, u_ref, gq_ref, gk_ref, gv_ref, lx_ref) = refs
    x = x_ref[...]
    h = _modulate(x, g_ref[2:3], mod_ref[0], 1).astype(BF16)

    def proj(lo, n):
        return jnp.dot(h, w_ref[:, lo:lo + n].astype(BF16), preferred_element_type=F32)

    naq_ref[...] = (proj(OFF_NA, NA_W) * ATTN_SCALE).astype(BF16)
    nak_ref[...] = proj(OFF_NA + NA_W, NA_W).astype(nak_ref.dtype)
    nav_ref[...] = proj(OFF_NA + 2 * NA_W, NA_W).astype(nav_ref.dtype)
    glu_a = proj(OFF_CONV, CONV_C)
    glu_g = proj(OFF_CONV + CONV_C, CONV_C)
    u_ref[...] = glu_a * jax.nn.sigmoid(glu_g)
    q = _head_rmsnorm(proj(OFF_GQA, GQA_QW), qn_ref[...], oq_ref[...])
    k = _head_rmsnorm(proj(OFF_GQA + GQA_QW, GQA_KVW), kn_ref[...], ok_ref[...])
    v = proj(OFF_GQA + GQA_QW + GQA_KVW, GQA_KVW)
    if latent:
        cos_t = cos_ref[...]
        sin_t = sin_ref[...]
        q = _rope(q, jnp.concatenate([cos_t] * (GQA_QW // LANES), axis=1),
                  jnp.concatenate([sin_t] * (GQA_QW // LANES), axis=1))
        k = _rope(k, cos_t, sin_t)
        gk_ref[...] = _dup_halves(k).astype(BF16)
        gv_ref[...] = _dup_halves(v).astype(BF16)
    else:
        gk_ref[...] = k
        gv_ref[...] = v
    gq_ref[...] = (q * ATTN_SCALE).astype(BF16)
    lx_ref[...] = proj(OFF_LRU, LRU_W)


def _proj(x, mod, g, w, qn, kn, ones_q, ones_k, rope_tabs, *, layer, latent, tokens_per_mod):
    T = x.shape[0]
    tm = TM_PROJ
    row = lambda n: pl.BlockSpec((tm, n), lambda i: (i, 0))
    in_specs = [
        row(D_MODEL),
        pl.BlockSpec((1, 9, D_MODEL), lambda i: (i * tm // tokens_per_mod, 0, 0)),
        _resident((6, D_MODEL)),
        _stacked((layer,), (D_MODEL, OFF_GATE)),
        _resident((1, GQA_QW)),
        _resident((1, GQA_KVW)),
        _resident((GQA_QW, GQA_QW)),
        _resident((GQA_KVW, GQA_KVW)),
    ]
    args = [x, mod, g, w, qn, kn, ones_q, ones_k]
    kv_dt = BF16 if latent else F32
    gkv_w = 2 * GQA_KVW if latent else GQA_KVW
    if latent:
        pos_blocks = DEC_SEQ // tm
        in_specs += [pl.BlockSpec((tm, LANES), lambda i: (i % pos_blocks, 0))] * 2
        args += list(rope_tabs)
    out_shape = [
        jax.ShapeDtypeStruct((T, NA_W), BF16),
        jax.ShapeDtypeStruct((T, NA_W), kv_dt),
        jax.ShapeDtypeStruct((T, NA_W), kv_dt),
        jax.ShapeDtypeStruct((T, CONV_C), F32),
        jax.ShapeDtypeStruct((T, GQA_QW), BF16),
        jax.ShapeDtypeStruct((T, gkv_w), kv_dt),
        jax.ShapeDtypeStruct((T, gkv_w), kv_dt),
        jax.ShapeDtypeStruct((T, LRU_W), F32),
    ]
    out_specs = [row(s.shape[1]) for s in out_shape]
    return pl.pallas_call(
        functools.partial(_proj_kernel, latent=latent),
        grid=(T // tm,),
        in_specs=in_specs,
        out_specs=out_specs,
        out_shape=out_shape,
        compiler_params=_cparams(("arbitrary",)),
        name="proj_latent" if latent else "proj_context",
    )(*args)


def _lane_lo(shape):
    return lax.broadcasted_iota(jnp.int32, shape, 1) < HEAD_DIM


def _qk(q, k):
    return lax.dot_general(q, k, (((1,), (1,)), ((), ())), preferred_element_type=F32)


def _values_with_ones(v):
    lo = _lane_lo(v.shape)
    ones = jnp.ones_like(v)
    return jnp.where(lo, v, ones), jnp.where(lo, ones, v)


def _pair_attention_pipeline(work):
    heads = [(item, half) for item in work for half in range(2)]

    def scores_of(item, half):
        load_q, keys, _, bias_fn, _ = item
        q2 = load_q()
        lo = _lane_lo(q2.shape)
        qm = jnp.where(lo if half == 0 else jnp.logical_not(lo), q2, jnp.zeros_like(q2))
        scores = [_qk(qm, k) for k in keys]
        if bias_fn is not None:
            scores[0] = scores[0] + bias_fn(half)
        return scores

    def softmax_of(scores):
        m = functools.reduce(jnp.maximum, [jnp.max(s, axis=-1, keepdims=True) for s in scores])
        return [jnp.exp2(s - m).astype(BF16) for s in scores]

    def output_of(item, half, ps):
        return functools.reduce(jnp.add, [jnp.dot(p, v, preferred_element_type=F32)
                                          for p, v in zip(ps, item[2][half])])

    def finish(item, raw0, raw1):
        lo = _lane_lo(raw0.shape)
        denom = pltpu.roll(jnp.where(lo, raw1, raw0), HEAD_DIM, axis=1)
        item[4](jnp.where(lo, raw0, raw1) * (1.0 / denom))

    n = len(heads)
    scores, probs, raw = {}, {}, {}
    for t in range(n + 4):
        if t < n:
            scores[t] = scores_of(*heads[t])
        if 0 <= t - 1 < n:
            probs[t - 1] = softmax_of(scores.pop(t - 1))
        if 0 <= t - 2 < n:
            item, half = heads[t - 2]
            raw[t - 2] = output_of(item, half, probs.pop(t - 2))
        i = t - 4
        if 0 <= i < n and i % 2 == 0:
            finish(heads[i][0], raw.pop(i), raw.pop(i + 1))


def _pair_attention_small(q2, k2, v2):
    lo = _lane_lo(q2.shape)
    outs = []
    for half in range(2):
        qm = jnp.where(lo if half == 0 else jnp.logical_not(lo), q2, jnp.zeros_like(q2))
        s = _qk(qm, k2)
        p = jnp.exp2(s - jnp.max(s, axis=-1, keepdims=True))
        l = jnp.sum(p, axis=-1, keepdims=True)
        outs.append(jnp.dot(p.astype(BF16), v2, preferred_element_type=F32) * (1.0 / l))
    return jnp.where(lo, outs[0], outs[1])


def _ctx_attn_kernel(naq_ref, nak_ref, nav_ref, gq_ref, gk_ref, gv_ref, ona_ref, ogq_ref):
    gk = _dup_halves(gk_ref[...]).astype(BF16)
    gv = _dup_halves(gv_ref[...]).astype(BF16)
    for j in range(HEAD_PAIRS):
        sl = slice(j * LANES, (j + 1) * LANES)
        kvs = slice((j // 2) * LANES, (j // 2 + 1) * LANES)
        ona_ref[:, sl] = _pair_attention_small(
            naq_ref[:, sl], nak_ref[:, sl].astype(BF16), nav_ref[:, sl].astype(BF16)).astype(BF16)
        ogq_ref[:, sl] = _pair_attention_small(gq_ref[:, sl], gk[:, kvs], gv[:, kvs]).astype(BF16)


def _ctx_attn(naq, nak, nav, gq, gk, gv):
    T = naq.shape[0]
    row = lambda n: pl.BlockSpec((SEQ, n), lambda b: (b, 0))
    return pl.pallas_call(
        _ctx_attn_kernel,
        grid=(T // SEQ,),
        in_specs=[row(NA_W), row(NA_W), row(NA_W), row(GQA_QW), row(GQA_KVW), row(GQA_KVW)],
        out_specs=[row(NA_W), row(GQA_QW)],
        out_shape=[jax.ShapeDtypeStruct((T, NA_W), BF16), jax.ShapeDtypeStruct((T, GQA_QW), BF16)],
        compiler_params=_cparams(("arbitrary",)),
        name="ctx_attn",
    )(naq, nak, nav, gq, gk, gv)


def _na_row_start(r):
    return min(max(r - NA_WIN_R // 2, 0), GRID_ROWS - NA_WIN_R)


def _na_bias_index(rq, rk):
    r0 = _na_row_start(rq)
    if r0 <= rk < r0 + NA_WIN_R:
        return rk - rq + NA_WIN_R - 1
    return None


def _na_latent_kernel(q_ref, k_ref, v_ref, ck_ref, cv_ref, bias_ref, o_ref):
    ck = ck_ref[0, 0].astype(BF16)
    cv = _values_with_ones(cv_ref[0, 0].astype(BF16))
    v = _values_with_ones(v_ref[...])
    qrows = NA_QROWS * GRID_W
    lo = _lane_lo((GRID_W, LANES))
    masked = jnp.full((GRID_W, LANES), NEG_INF, F32)
    work = []
    for qb in range(GRID_ROWS // NA_QROWS):
        w0 = min(max(NA_QROWS * qb - NA_WIN_R // 2, 0), GRID_ROWS - NA_WROWS)
        w0 -= w0 % 2
        win = slice(w0 * GRID_W, (w0 + NA_WROWS) * GRID_W)
        kwin = k_ref[win, :]
        values = tuple([v[half][win], cv[half]] for half in range(2))

        def bias_fn(half, qb=qb, w0=w0):
            rows = []
            for qi in range(NA_QROWS):
                rq = NA_QROWS * qb + qi
                blocks = []
                for kp in range(NA_WROWS // 2):
                    ia = _na_bias_index(rq, w0 + 2 * kp)
                    ib = _na_bias_index(rq, w0 + 2 * kp + 1)
                    blk_a = masked if ia is None else bias_ref[half, ia]
                    blk_b = masked if ib is None else bias_ref[half, ib]
                    blocks.append(masked if ia is None and ib is None else jnp.where(lo, blk_a, blk_b))
                rows.append(jnp.concatenate(blocks, axis=1))
            return jnp.concatenate(rows, axis=0)

        def store(o, qb=qb):
            o_ref[qb * qrows:(qb + 1) * qrows, :] = o.astype(BF16)

        work.append((lambda qb=qb: q_ref[qb * qrows:(qb + 1) * qrows, :], [kwin, ck], values, bias_fn, store))
    _pair_attention_pipeline(work)


def _na_latent(q, k, v, cache_k, cache_v, bias_tab, layer):
    tok = pl.BlockSpec((DEC_SEQ, LANES), lambda b, j: (b, j))
    cache = pl.BlockSpec((1, 1, PAST_LEN, LANES), lambda b, j: (b, layer, 0, j))
    return pl.pallas_call(
        _na_latent_kernel,
        grid=(DEC_BATCH, HEAD_PAIRS),
        in_specs=[tok, tok, tok, cache, cache,
                  pl.BlockSpec((None, 2, 2 * NA_WIN_R - 1, GRID_W, LANES), lambda b, j: (layer, j, 0, 0, 0))],
        out_specs=tok,
        out_shape=jax.ShapeDtypeStruct(q.shape, BF16),
        compiler_params=_cparams(("arbitrary", "arbitrary")),
        name="na_latent",
    )(q, k, v, cache_k, cache_v, bias_tab)


def _gqa_latent_kernel(q_ref, k_ref, v_ref, ck_ref, cv_ref, o_ref):
    group = pl.program_id(1) // (HEAD_PAIRS // GQA_KV_HEADS)

    def both_halves(x):
        lane = lax.broadcasted_iota(jnp.int32, x.shape, 1)
        keep = jnp.where(lane < HEAD_DIM, 0, 1) == group
        return jnp.where(keep, x, pltpu.roll(x, HEAD_DIM, axis=1)).astype(BF16)

    ck = both_halves(ck_ref[0, 0])
    cv = _values_with_ones(both_halves(cv_ref[0, 0]))
    k = k_ref[...]
    v = _values_with_ones(v_ref[...])
    values = tuple([v[half], cv[half]] for half in range(2))
    work = []
    for qb in range(DEC_SEQ // GQA_QBLK):
        sl = slice(qb * GQA_QBLK, (qb + 1) * GQA_QBLK)

        def store(o, sl=sl):
            o_ref[sl, :] = o.astype(BF16)

        work.append((lambda sl=sl: q_ref[sl, :], [k, ck], values, None, store))
    _pair_attention_pipeline(work)


def _gqa_latent(q, k, v, cache_k, cache_v, layer):
    tok = pl.BlockSpec((DEC_SEQ, LANES), lambda b, j: (b, j))
    kv = pl.BlockSpec((DEC_SEQ, LANES), lambda b, j: (b, j // (HEAD_PAIRS // GQA_KV_HEADS)))
    cache = pl.BlockSpec((1, 1, PAST_LEN, GQA_KVW), lambda b, j: (b, layer, 0, 0))
    return pl.pallas_call(
        _gqa_latent_kernel,
        grid=(DEC_BATCH, HEAD_PAIRS),
        in_specs=[tok, kv, kv, cache, cache],
        out_specs=tok,
        out_shape=jax.ShapeDtypeStruct(q.shape, BF16),
        compiler_params=_cparams(("arbitrary", "arbitrary")),
        name="gqa_latent",
    )(q, k, v, cache_k, cache_v)


def _conv_kernel(u_ref, dw_ref, g_ref, b_ref, o_ref, pad_ref, *, L):
    zeros = jnp.zeros((CONV_PAD, CONV_C), F32)
    pad_ref[0:CONV_PAD, :] = zeros
    pad_ref[CONV_PAD + L:2 * CONV_PAD + L, :] = zeros
    pad_ref[CONV_PAD:CONV_PAD + L, :] = u_ref[...]
    first = CONV_PAD - CONV_W // 2

    def chunk(c, carry):
        base = pl.multiple_of(c * CONV_ROWS, CONV_ROWS)
        cols = []
        for lc in range(CONV_C // LANES):
            ls = slice(lc * LANES, (lc + 1) * LANES)
            acc = None
            for r in range(SUBLANES):
                z = None
                for k in range(CONV_W):
                    if (first + k) % SUBLANES != r:
                        continue
                    aligned = first + k - r
                    term = dw_ref[k:k + 1, ls] * pad_ref[pl.ds(base + aligned, CONV_ROWS + SUBLANES), ls]
                    z = term if z is None else z + term
                z = z[r:r + CONV_ROWS]
                acc = z if acc is None else acc + z
            cols.append(acc)
        acc = jnp.concatenate(cols, axis=1)
        mu = jnp.mean(acc, axis=-1, keepdims=True)
        d = acc - mu
        var = jnp.mean(d * d, axis=-1, keepdims=True)
        y = d * lax.rsqrt(var + EPS) * g_ref[...] + b_ref[...]
        o_ref[pl.ds(base, CONV_ROWS), :] = (y * jax.nn.sigmoid(y)).astype(BF16)
        return carry

    lax.fori_loop(0, L // CONV_ROWS, chunk, 0)


def _conv_branch(u, dw, ln_g, ln_b, *, L):
    T = u.shape[0]
    return pl.pallas_call(
        functools.partial(_conv_kernel, L=L),
        grid=(T // L,),
        in_specs=[pl.BlockSpec((L, CONV_C), lambda b: (b, 0)), _resident((CONV_W, CONV_C)),
                  _resident((1, CONV_C)), _resident((1, CONV_C))],
        out_specs=pl.BlockSpec((L, CONV_C), lambda b: (b, 0)),
        out_shape=jax.ShapeDtypeStruct((T, CONV_C), BF16),
        scratch_shapes=[pltpu.VMEM((L + 2 * CONV_PAD, CONV_C), F32)],
        compiler_params=_cparams(("arbitrary",)),
        name="conv_branch",
    )(u, dw, ln_g, ln_b)


def _sigmoid_tanh(x):
    return 0.5 * jnp.tanh(0.5 * x) + 0.5


def _lru_kernel(x_ref, cw_ref, cb_ref, wg_ref, bg_ref, lam_ref, h0_ref, y_ref, st_ref,
                pad_ref, a_ref, b_ref, *, L):
    zeros = jnp.zeros((LRU_PAD, LRU_W), F32)
    first = LRU_PAD - LRU_CONV_W // 2
    decay = LRU_C * jax.nn.log_sigmoid(lam_ref[...])
    for s in range(LRU_NB):
        pad_ref[s, 0:LRU_PAD, :] = zeros
        pad_ref[s, LRU_PAD + L:2 * LRU_PAD + L, :] = zeros
        pad_ref[s, LRU_PAD:LRU_PAD + L, :] = x_ref[s * L:(s + 1) * L, :]
        for c in range(L // LRU_GATE_ROWS):
            r0 = c * LRU_GATE_ROWS
            rows = slice(s * L + r0, s * L + r0 + LRU_GATE_ROWS)
            xc = cb_ref[...] + functools.reduce(jnp.add, [
                cw_ref[k:k + 1, :] * pad_ref[s, r0 + first + k:r0 + first + k + LRU_GATE_ROWS, :]
                for k in range(LRU_CONV_W)])
            xcb = xc.astype(BF16)

            def gate(d, which):
                z = jnp.concatenate([
                    jnp.dot(xcb[:, :LRU_HALF], wg_ref[d, which, 0], preferred_element_type=F32),
                    jnp.dot(xcb[:, LRU_HALF:], wg_ref[d, which, 1], preferred_element_type=F32)], axis=1)
                return _sigmoid_tanh(z + bg_ref[d, which:which + 1, :])

            for d in range(2):
                log_a = decay[d:d + 1, :] * gate(d, 0)
                a = jnp.exp(log_a)
                one_minus_a2 = -jnp.tanh(log_a) * (a * a + 1.0)
                a_ref[d, rows, :] = a
                b_ref[d, rows, :] = jnp.sqrt(one_minus_a2) * (gate(d, 1) * xc)

    def block(i, carry):
        carry = list(carry)
        fwd = pl.multiple_of(i * SUBLANES, SUBLANES)
        bwd = pl.multiple_of(L - SUBLANES - i * SUBLANES, SUBLANES)
        for j in range(SUBLANES):
            for s in range(LRU_NB):
                hf, hb = carry[2 * s], carry[2 * s + 1]
                tf = pl.ds(s * L + fwd + j, 1)
                tb = pl.ds(s * L + bwd + (SUBLANES - 1 - j), 1)
                hf = a_ref[0, tf, :] * hf + b_ref[0, tf, :]
                hb = a_ref[1, tb, :] * hb + b_ref[1, tb, :]
                b_ref[0, tf, :] = hf
                b_ref[1, tb, :] = hb
                carry[2 * s], carry[2 * s + 1] = hf, hb
        return tuple(carry)

    init = []
    for s in range(LRU_NB):
        h0 = h0_ref[s]
        init += [h0[0:1, :], h0[1:2, :]]
    final = lax.fori_loop(0, L // SUBLANES, block, tuple(init))
    y_ref[...] = (b_ref[0] + b_ref[1]).astype(BF16)
    for s in range(LRU_NB):
        st_ref[s] = jnp.concatenate([final[2 * s], final[2 * s + 1]], axis=0)


def _lru_branch(x, cw, cb, wg, bg, lam, h0, *, L):
    T = x.shape[0]
    nb = T // L
    rows = LRU_NB * L
    return pl.pallas_call(
        functools.partial(_lru_kernel, L=L),
        grid=(nb // LRU_NB,),
        in_specs=[pl.BlockSpec((rows, LRU_W), lambda b: (b, 0)), _resident((LRU_CONV_W, LRU_W)),
                  _resident((1, LRU_W)), _resident((2, 2, 2, LRU_HALF, LRU_HALF)),
                  _resident((2, 2, LRU_W)), _resident((2, LRU_W)),
                  pl.BlockSpec((LRU_NB, 2, LRU_W), lambda b: (b, 0, 0))],
        out_specs=[pl.BlockSpec((rows, LRU_W), lambda b: (b, 0)),
                   pl.BlockSpec((LRU_NB, 2, LRU_W), lambda b: (b, 0, 0))],
        out_shape=[jax.ShapeDtypeStruct((T, LRU_W), BF16), jax.ShapeDtypeStruct((nb, 2, LRU_W), F32)],
        scratch_shapes=[pltpu.VMEM((LRU_NB, L + 2 * LRU_PAD, LRU_W), F32), pltpu.VMEM((2, rows, LRU_W), F32),
                        pltpu.VMEM((2, rows, LRU_W), F32)],
        compiler_params=_cparams(("arbitrary",)),
        name="lru_branch",
    )(x, cw, cb, wg, bg, lam, h0)


def _merge_tile(x_ref, mod_ref, g_ref, wga_ref, wgb_ref, b0_ref, b1_ref, b2_ref, b3_ref, wb_ref, wo_ref, o_ref):
    x = x_ref[...]
    mod = mod_ref[0]
    h = _modulate(x, g_ref[2:3], mod, 1).astype(BF16)

    def gate_logits(lo, hi):
        parts = []
        if lo < GATE_A:
            parts.append(jnp.dot(h, wga_ref[:, lo:min(hi, GATE_A)].astype(BF16), preferred_element_type=F32))
        if hi > GATE_A:
            parts.append(jnp.dot(h, wgb_ref[:, max(lo, GATE_A) - GATE_A:hi - GATE_A].astype(BF16),
                                 preferred_element_type=F32))
        return parts[0] if len(parts) == 1 else jnp.concatenate(parts, axis=1)

    merged = None
    for k, br in enumerate((b0_ref, b1_ref, b2_ref, b3_ref)):
        logits = gate_logits(k * D_MODEL, (k + 1) * D_MODEL)
        term = jax.nn.sigmoid(logits) * jnp.dot(br[...], wb_ref[k].astype(BF16), preferred_element_type=F32)
        merged = term if merged is None else merged + term
    y = jnp.dot(merged.astype(BF16), wo_ref[...].astype(BF16), preferred_element_type=F32)
    o_ref[...] = x + mod[5:6] * (_rms(y) * g_ref[3:4])


def _merge_kernel(xp_ref, xs_ref, mod_ref, g_ref, wga_ref, wgb_ref, bp0, bp1, bp2, bp3, bs0, bs1, bs2, bs3,
                  wb_ref, wo_ref, op_ref, os_ref, *, prompt_tiles):
    is_prompt = pl.program_id(0) < prompt_tiles

    @pl.when(is_prompt)
    def _():
        _merge_tile(xp_ref, mod_ref, g_ref, wga_ref, wgb_ref, bp0, bp1, bp2, bp3, wb_ref, wo_ref, op_ref)

    @pl.when(jnp.logical_not(is_prompt))
    def _():
        _merge_tile(xs_ref, mod_ref, g_ref, wga_ref, wgb_ref, bs0, bs1, bs2, bs3, wb_ref, wo_ref, os_ref)


def _merge(xp, xs, mod, g, w_in, branches_p, branches_s, w_branch, w_out, *, layer):
    tm = TM_MERGE
    nbp = xp.shape[0] // tm
    nbs = xs.shape[0] // tm
    prompt_tile = lambda i: (jnp.minimum(i, nbp - 1), 0)
    sample_tile = lambda i: (jnp.maximum(i - nbp, 0), 0)
    return pl.pallas_call(
        functools.partial(_merge_kernel, prompt_tiles=nbp),
        grid=(nbp + nbs,),
        in_specs=[pl.BlockSpec((tm, D_MODEL), prompt_tile), pl.BlockSpec((tm, D_MODEL), sample_tile),
                  pl.BlockSpec((1, 9, D_MODEL), lambda i: (i * tm // DEC_SEQ, 0, 0)),
                  _resident((6, D_MODEL)),
                  pl.BlockSpec((None, D_MODEL, GATE_A), lambda i: (layer, 0, OFF_GATE // GATE_A),
                               pipeline_mode=pl.Buffered(1)),
                  pl.BlockSpec((None, D_MODEL, GATE_B), lambda i: (layer, 0, (OFF_GATE + GATE_A) // GATE_B),
                               pipeline_mode=pl.Buffered(1))]
        + [pl.BlockSpec((tm, BRANCH_W), prompt_tile)] * N_BRANCH
        + [pl.BlockSpec((tm, BRANCH_W), sample_tile)] * N_BRANCH
        + [_stacked((layer,), (N_BRANCH, BRANCH_W, D_MODEL)), _stacked((layer,), (D_MODEL, D_MODEL))],
        out_specs=[pl.BlockSpec((tm, D_MODEL), prompt_tile), pl.BlockSpec((tm, D_MODEL), sample_tile)],
        out_shape=[jax.ShapeDtypeStruct(xp.shape, F32), jax.ShapeDtypeStruct(xs.shape, F32)],
        compiler_params=_cparams(("arbitrary",)),
        name="merge",
    )(xp, xs, mod, g, w_in, w_in, *branches_p, *branches_s, w_branch, w_out)


def _rope_tables():
    t = np.arange(DEC_SEQ)
    row = (t // GRID_W).astype(np.float32)
    col = (t % GRID_W).astype(np.float32)
    half = HEAD_DIM // 2
    freqs = (np.float32(ROPE_THETA) ** (-np.arange(0, half, 2, dtype=np.float32) / np.float32(half))).astype(np.float32)
    ar = row[:, None] * freqs
    ac = col[:, None] * freqs
    cos_h = np.concatenate([np.cos(ar), np.cos(ar), np.cos(ac), np.cos(ac)], axis=1)
    sin_h = np.concatenate([-np.sin(ar), np.sin(ar), -np.sin(ac), np.sin(ac)], axis=1)
    return (jnp.asarray(np.tile(cos_h, (1, 2)), F32), jnp.asarray(np.tile(sin_h, (1, 2)), F32))


def _na_bias_tables(na_rpb):
    c = np.arange(GRID_W)
    c0 = np.clip(c - NA_WIN_C // 2, 0, GRID_W - NA_WIN_C)
    col_ok = (c[None, :] >= c0[:, None]) & (c[None, :] < c0[:, None] + NA_WIN_C)
    dc = np.clip(c[None, :] - c[:, None], -(NA_WIN_C - 1), NA_WIN_C - 1) + NA_WIN_C - 1
    col_ok = np.concatenate([col_ok, col_ok], axis=1)
    dc = np.concatenate([dc, dc], axis=1)
    onehot = (np.arange(2 * NA_WIN_C - 1)[:, None, None] == dc[None]) & col_ok[None]
    tab = jnp.einsum('lhrm,mqk->lhrqk', na_rpb.astype(F32) * LOG2E, jnp.asarray(onehot, F32),
                     precision=lax.Precision.HIGHEST)
    return tab + jnp.asarray(np.where(col_ok, 0.0, NEG_INF), F32)


def _block_diag_ones(n):
    return jnp.asarray(np.kron(np.eye(n // HEAD_DIM), np.ones((HEAD_DIM, HEAD_DIM))), BF16)


def _lru_gate_weights(lru_wr, lru_wi):
    w = jnp.stack([lru_wr, lru_wi], axis=2)
    per_half = LRU_BLOCKS // 2
    bw = LRU_W // LRU_BLOCKS
    w = w.reshape(DEPTH, 2, 2, 2, per_half, bw, bw)
    eye = jnp.asarray(np.eye(per_half), F32)
    bd = jnp.einsum('...nij,nm->...nimj', w, eye).reshape(DEPTH, 2, 2, 2, LRU_HALF, LRU_HALF)
    return bd.astype(BF16)


def kernel(x_prompt, x_sample, c, cache_na_k, cache_na_v, cache_gqa_k, cache_gqa_v, state_lru, c_ctx, w_ada, b_ada, norm_g, ffn_w1, ffn_w2, w_in, na_rpb, conv_dw, conv_ln_g, conv_ln_b, gqa_q_norm, gqa_k_norm, lru_conv_w, lru_conv_b, lru_wr, lru_br, lru_wi, lru_bi, lru_lambda, w_branch, w_out):
    tp = BATCH * SEQ
    ts = DEC_BATCH * DEC_SEQ
    c_all = jnp.concatenate([c_ctx[None, :], c, jnp.zeros((ADA_ROWS - 1 - DEC_BATCH, D_MODEL), F32)], axis=0)
    mods = _adaln(c_all, w_ada, b_ada).reshape(DEPTH, ADA_ROWS, 9, D_MODEL)

    w1, w2, w_in16, wb, wo = ffn_w1, ffn_w2, w_in, w_branch, w_out
    lru_wg = _lru_gate_weights(lru_wr, lru_wi)
    lru_bg = jnp.stack([lru_br, lru_bi], axis=2)
    bias_tab = _na_bias_tables(na_rpb)
    rope_tabs = _rope_tables()
    ones_q = _block_diag_ones(GQA_QW)
    ones_k = _block_diag_ones(GQA_KVW)
    cna_k = cache_na_k.reshape(DEC_BATCH, DEPTH, PAST_LEN, NA_W)
    cna_v = cache_na_v.reshape(DEC_BATCH, DEPTH, PAST_LEN, NA_W)
    cgq_k = cache_gqa_k.reshape(DEC_BATCH, DEPTH, PAST_LEN, GQA_KVW)
    cgq_v = cache_gqa_v.reshape(DEC_BATCH, DEPTH, PAST_LEN, GQA_KVW)
    h0_prompt = jnp.zeros((BATCH, 2, LRU_W), F32)

    xp = x_prompt.reshape(tp, D_MODEL)
    xs = x_sample.reshape(ts, D_MODEL)
    nk_l, nv_l, gk_l, gv_l, st_l = [], [], [], [], []
    for l in range(DEPTH):
        g = norm_g[l]
        mod_p = mods[l, 0:1]
        mod_s = mods[l, 1:1 + DEC_BATCH]
        qn = jnp.tile(gqa_q_norm[l], GQA_HEADS)[None, :]
        kn = jnp.tile(gqa_k_norm[l], GQA_KV_HEADS)[None, :]
        lru_args = (lru_conv_w[l], lru_conv_b[l][None, :], lru_wg[l], lru_bg[l], lru_lambda[l])
        conv_args = (conv_dw[l], conv_ln_g[l][None, :], conv_ln_b[l][None, :])

        mod_all = jnp.concatenate([jnp.broadcast_to(mod_p, (tp // DEC_SEQ, 9, D_MODEL)), mod_s], axis=0)

        xp, xs = _ffn(xp, xs, mod_all, g, w1, w2, layer=l, slot=0)

        naq, nak, nav, u, gq, gk, gv, lx = _proj(xp, mod_p, g, w_in16, qn, kn, ones_q, ones_k, None,
                                                  layer=l, latent=False, tokens_per_mod=tp)
        o_na, o_gq = _ctx_attn(naq, nak, nav, gq, gk, gv)
        o_conv = _conv_branch(u, *conv_args, L=SEQ)
        o_lru, st = _lru_branch(lx, *lru_args, h0_prompt, L=SEQ)
        branches_p = (o_na, o_conv, o_gq, o_lru)
        nk_l.append(nak.reshape(BATCH, SEQ, NA_HEADS, HEAD_DIM))
        nv_l.append(nav.reshape(BATCH, SEQ, NA_HEADS, HEAD_DIM))
        gk_l.append(gk.reshape(BATCH, SEQ, GQA_KV_HEADS, HEAD_DIM))
        gv_l.append(gv.reshape(BATCH, SEQ, GQA_KV_HEADS, HEAD_DIM))
        st_l.append(st)

        naq, nak, nav, u, gq, gk, gv, lx = _proj(xs, mod_s, g, w_in16, qn, kn, ones_q, ones_k, rope_tabs,
                                                  layer=l, latent=True, tokens_per_mod=DEC_SEQ)
        o_na = _na_latent(naq, nak, nav, cna_k, cna_v, bias_tab, l)
        o_gq = _gqa_latent(gq, gk, gv, cgq_k, cgq_v, l)
        o_conv = _conv_branch(u, *conv_args, L=DEC_SEQ)
        o_lru, _ = _lru_branch(lx, *lru_args, state_lru[:, l], L=DEC_SEQ)
        branches_s = (o_na, o_conv, o_gq, o_lru)

        xp, xs = _merge(xp, xs, mod_all, g, w_in16, branches_p, branches_s, wb, wo, layer=l)
        xp, xs = _ffn(xp, xs, mod_all, g, w1, w2, layer=l, slot=1)

    return (xp.reshape(BATCH, SEQ, D_MODEL), xs.reshape(DEC_BATCH, DEC_SEQ, D_MODEL),
            jnp.stack(nk_l, axis=1), jnp.stack(nv_l, axis=1), jnp.stack(gk_l, axis=1), jnp.stack(gv_l, axis=1),
            jnp.stack(st_l, axis=1))
```

```python
import functools

import numpy as np
import jax
import jax.numpy as jnp
from jax import lax
from jax.experimental import pallas as pl
from jax.experimental.pallas import tpu as pltpu

D_MODEL = 1024
BATCH = 16
SEQ = 256
DEPTH = 4
DEC_BATCH = 8
DEC_SEQ = 1024
PAST_LEN = 512
GRID_W = 64
GRID_ROWS = DEC_SEQ // GRID_W
HEAD_DIM = 64
NA_HEADS = 8
NA_WIN_R = 8
NA_WIN_C = 16
CONV_C = 512
CONV_W = 31
GQA_HEADS = 8
GQA_KV_HEADS = 2
LRU_W = 512
LRU_BLOCKS = 8
LRU_CONV_W = 4
LRU_C = 8.0
N_BRANCH = 4
BRANCH_W = 512
D_FF = 2816
ROPE_THETA = 10000.0
EPS = 1e-6
NEG_INF = -1e30

NA_W = NA_HEADS * HEAD_DIM
GQA_QW = GQA_HEADS * HEAD_DIM
GQA_KVW = GQA_KV_HEADS * HEAD_DIM
OFF_NA = 0
OFF_CONV = OFF_NA + 3 * NA_W
OFF_GQA = OFF_CONV + 2 * CONV_C
OFF_LRU = OFF_GQA + GQA_QW + 2 * GQA_KVW
OFF_GATE = OFF_LRU + LRU_W
IN_COLS = OFF_GATE + N_BRANCH * D_MODEL
N_MOD = 9 * D_MODEL
GATE_A = OFF_GATE
GATE_B = IN_COLS - OFF_GATE - GATE_A
assert GATE_B > 0 and OFF_GATE % GATE_A == 0 and (OFF_GATE + GATE_A) % GATE_B == 0 and GATE_B % 128 == 0

F32 = jnp.float32
BF16 = jnp.bfloat16

LANES = 128
SUBLANES = 8
HEAD_PAIRS = NA_HEADS // 2
LOG2E = 1.4426950408889634
ATTN_SCALE = HEAD_DIM ** -0.5 * LOG2E
ADA_ROWS = 16
VMEM_LIMIT = 56 * 1024 * 1024

TM_FFN = 512
TM_PROJ = 256
TM_MERGE = 256
FF_CHUNK = 256
ADA_TN = 1152
CONV_ROWS = 128
CONV_PAD = 16
LRU_PAD = 8
LRU_GATE_ROWS = 256
LRU_HALF = LRU_W // 2
LRU_NB = 2
NA_QROWS = 2
NA_WROWS = 10
GQA_QBLK = 256


def _cparams(sem):
    return pltpu.CompilerParams(dimension_semantics=sem, vmem_limit_bytes=VMEM_LIMIT)


def _rms(x):
    return x * lax.rsqrt(jnp.mean(x * x, axis=-1, keepdims=True) + EPS)


def _modulate(x, g_pre, mod, sub):
    shift = mod[3 * sub:3 * sub + 1]
    scale = mod[3 * sub + 1:3 * sub + 2]
    return (_rms(x) * g_pre) * (1.0 + scale) + shift


def _stacked(idx, shape):
    idx = tuple(idx)
    return pl.BlockSpec((None,) * len(idx) + tuple(shape), lambda *_: idx + (0,) * len(shape),
                        pipeline_mode=pl.Buffered(1))


def _resident(shape):
    return _stacked((), shape)


def _adaln_kernel(c_ref, w_ref, b_ref, o_ref):
    c = c_ref[...]
    s = (c * jax.nn.sigmoid(c)).astype(BF16)
    o_ref[0] = jnp.dot(s, w_ref[0].astype(BF16), preferred_element_type=F32) + b_ref[0]


def _adaln(c_all, w_ada, b_ada):
    return pl.pallas_call(
        _adaln_kernel,
        grid=(DEPTH, N_MOD // ADA_TN),
        in_specs=[
            pl.BlockSpec((ADA_ROWS, D_MODEL), lambda l, n: (0, 0)),
            pl.BlockSpec((1, D_MODEL, ADA_TN), lambda l, n: (l, 0, n)),
            pl.BlockSpec((1, 1, ADA_TN), lambda l, n: (l, 0, n)),
        ],
        out_specs=pl.BlockSpec((1, ADA_ROWS, ADA_TN), lambda l, n: (l, 0, n)),
        out_shape=jax.ShapeDtypeStruct((DEPTH, ADA_ROWS, N_MOD), F32),
        compiler_params=_cparams(("arbitrary", "arbitrary")),
        name="adaln",
    )(c_all, w_ada, b_ada.reshape(DEPTH, 1, N_MOD))


def _ffn_tile(x_ref, mod_ref, g_ref, w1_ref, w2_ref, o_ref, acc_ref, *, sub, res_w):
    x = x_ref[...]
    mod = mod_ref[0]
    h = _modulate(x, g_ref[2 * sub:2 * sub + 1], mod, sub).astype(BF16)
    for c in range(D_FF // FF_CHUNK):
        lo = c * FF_CHUNK
        a = jnp.dot(h, w1_ref[:, lo:lo + FF_CHUNK].astype(BF16), preferred_element_type=F32)
        u = jnp.dot(h, w1_ref[:, D_FF + lo:D_FF + lo + FF_CHUNK].astype(BF16), preferred_element_type=F32)
        act = (a * jax.nn.sigmoid(a) * u).astype(BF16)
        part = jnp.dot(act, w2_ref[lo:lo + FF_CHUNK, :].astype(BF16), preferred_element_type=F32)
        if c == 0:
            acc_ref[...] = part
        else:
            acc_ref[...] += part
    gate = mod[3 * sub + 2:3 * sub + 3]
    o_ref[...] = x + (res_w * gate) * (_rms(acc_ref[...]) * g_ref[2 * sub + 1:2 * sub + 2])


def _ffn_kernel(xp_ref, xs_ref, mod_ref, g_ref, w1_ref, w2_ref, op_ref, os_ref, acc_ref, *, prompt_tiles, **kw):
    is_prompt = pl.program_id(0) < prompt_tiles

    @pl.when(is_prompt)
    def _():
        _ffn_tile(xp_ref, mod_ref, g_ref, w1_ref, w2_ref, op_ref, acc_ref, **kw)

    @pl.when(jnp.logical_not(is_prompt))
    def _():
        _ffn_tile(xs_ref, mod_ref, g_ref, w1_ref, w2_ref, os_ref, acc_ref, **kw)


def _ffn(xp, xs, mod, g, w1, w2, *, layer, slot):
    tm = TM_FFN
    nbp = xp.shape[0] // tm
    nbs = xs.shape[0] // tm
    prompt_tile = lambda i: (jnp.minimum(i, nbp - 1), 0)
    sample_tile = lambda i: (jnp.maximum(i - nbp, 0), 0)
    return pl.pallas_call(
        functools.partial(_ffn_kernel, sub=2 * slot, res_w=0.5, prompt_tiles=nbp),
        grid=(nbp + nbs,),
        in_specs=[
            pl.BlockSpec((tm, D_MODEL), prompt_tile),
            pl.BlockSpec((tm, D_MODEL), sample_tile),
            pl.BlockSpec((1, 9, D_MODEL), lambda i: (i * tm // DEC_SEQ, 0, 0)),
            _resident((6, D_MODEL)),
            _stacked((layer, slot), (D_MODEL, 2 * D_FF)),
            _stacked((layer, slot), (D_FF, D_MODEL)),
        ],
        out_specs=[pl.BlockSpec((tm, D_MODEL), prompt_tile), pl.BlockSpec((tm, D_MODEL), sample_tile)],
        out_shape=[jax.ShapeDtypeStruct(xp.shape, F32), jax.ShapeDtypeStruct(xs.shape, F32)],
        scratch_shapes=[pltpu.VMEM((tm, D_MODEL), F32)],
        compiler_params=_cparams(("arbitrary",)),
        name="ffn",
    )(xp, xs, mod, g, w1, w2)


def _head_sumsq(x, ones_bd):
    x2 = x * x
    hi = x2.astype(BF16)
    lo = (x2 - hi.astype(F32)).astype(BF16)
    return jnp.dot(hi, ones_bd, preferred_element_type=F32) + jnp.dot(lo, ones_bd, preferred_element_type=F32)


def _head_rmsnorm(x, gain, ones_bd):
    ss = _head_sumsq(x, ones_bd)
    return x * lax.rsqrt(ss * (1.0 / HEAD_DIM) + EPS) * gain


def _rope(x, cos_t, sin_t):
    n = x.shape[-1]
    lane = lax.broadcasted_iota(jnp.int32, x.shape, 1)
    first = jnp.bitwise_and(lane, 31) < 16
    partner = jnp.where(first, pltpu.roll(x, n - 16, axis=1), pltpu.roll(x, 16, axis=1))
    return x * cos_t + partner * sin_t


def _dup_halves(x):
    lane = lax.broadcasted_iota(jnp.int32, x.shape, 1)
    swapped = pltpu.roll(x, HEAD_DIM, axis=1)
    lo = lane < HEAD_DIM
    return jnp.concatenate([jnp.where(lo, x, swapped), jnp.where(lo, swapped, x)], axis=1)


def _proj_kernel(xp_ref, xs_ref, mod_ref, g_ref, w_ref, qn_ref, kn_ref, oq_ref, ok_ref, cos_ref, sin_ref,
                 *outs, prompt_tiles):
    is_prompt = pl.program_id(0) < prompt_tiles
    shared = (mod_ref, g_ref, w_ref, qn_ref, kn_ref, oq_ref, ok_ref, cos_ref, sin_ref)
    n_out = len(outs) // 2

    @pl.when(is_prompt)
    def _():
        _proj_tile(xp_ref, *shared, *outs[:n_out], latent=False)

    @pl.when(jnp.logical_not(is_prompt))
    def _():
        _proj_tile(xs_ref, *shared, *outs[n_out:], latent=True)


def _proj_tile(x_ref, mod_ref, g_ref, w_ref, qn_ref, kn_ref, oq_ref, ok_ref, cos_ref, sin_ref,
               naq_ref, nak_ref, nav_ref, u_ref, gq_ref, gk_ref, gv_ref, lx_ref, *, latent):
    x = x_ref[...]
    h = _modulate(x, g_ref[2:3], mod_ref[0], 1).astype(BF16)

    def proj(lo, n):
        return jnp.dot(h, w_ref[:, lo:lo + n].astype(BF16), preferred_element_type=F32)

    naq_ref[...] = (proj(OFF_NA, NA_W) * ATTN_SCALE).astype(BF16)
    nak_ref[...] = proj(OFF_NA + NA_W, NA_W).astype(nak_ref.dtype)
    nav_ref[...] = proj(OFF_NA + 2 * NA_W, NA_W).astype(nav_ref.dtype)
    glu_a = proj(OFF_CONV, CONV_C)
    glu_g = proj(OFF_CONV + CONV_C, CONV_C)
    u_ref[...] = glu_a * jax.nn.sigmoid(glu_g)
    q = _head_rmsnorm(proj(OFF_GQA, GQA_QW), qn_ref[...], oq_ref[...])
    k = _head_rmsnorm(proj(OFF_GQA + GQA_QW, GQA_KVW), kn_ref[...], ok_ref[...])
    v = proj(OFF_GQA + GQA_QW + GQA_KVW, GQA_KVW)
    if latent:
        cos_t = cos_ref[...]
        sin_t = sin_ref[...]
        q = _rope(q, jnp.concatenate([cos_t] * (GQA_QW // LANES), axis=1),
                  jnp.concatenate([sin_t] * (GQA_QW // LANES), axis=1))
        k = _rope(k, cos_t, sin_t)
        gk_ref[...] = _dup_halves(k).astype(BF16)
        gv_ref[...] = _dup_halves(v).astype(BF16)
    else:
        gk_ref[...] = k
        gv_ref[...] = v
    gq_ref[...] = (q * ATTN_SCALE).astype(BF16)
    lx_ref[...] = proj(OFF_LRU, LRU_W)


def _proj(xp, xs, mod, g, w, qn, kn, ones_q, ones_k, rope_tabs, *, layer):
    tm = TM_PROJ
    nbp = xp.shape[0] // tm
    nbs = xs.shape[0] // tm
    pos_blocks = DEC_SEQ // tm
    prompt_tile = lambda i: (jnp.minimum(i, nbp - 1), 0)
    sample_tile = lambda i: (jnp.maximum(i - nbp, 0), 0)
    in_specs = [
        pl.BlockSpec((tm, D_MODEL), prompt_tile),
        pl.BlockSpec((tm, D_MODEL), sample_tile),
        pl.BlockSpec((1, 9, D_MODEL), lambda i: (i * tm // DEC_SEQ, 0, 0)),
        _resident((6, D_MODEL)),
        _stacked((layer,), (D_MODEL, OFF_GATE)),
        _resident((1, GQA_QW)),
        _resident((1, GQA_KVW)),
        _resident((GQA_QW, GQA_QW)),
        _resident((GQA_KVW, GQA_KVW)),
    ] + [pl.BlockSpec((tm, LANES), lambda i: (jnp.maximum(i - nbp, 0) % pos_blocks, 0))] * 2

    def outputs(T, latent, tile):
        kv_dt = BF16 if latent else F32
        gkv_w = 2 * GQA_KVW if latent else GQA_KVW
        shapes = [
            jax.ShapeDtypeStruct((T, NA_W), BF16),
            jax.ShapeDtypeStruct((T, NA_W), kv_dt),
            jax.ShapeDtypeStruct((T, NA_W), kv_dt),
            jax.ShapeDtypeStruct((T, CONV_C), F32),
            jax.ShapeDtypeStruct((T, GQA_QW), BF16),
            jax.ShapeDtypeStruct((T, gkv_w), kv_dt),
            jax.ShapeDtypeStruct((T, gkv_w), kv_dt),
            jax.ShapeDtypeStruct((T, LRU_W), F32),
        ]
        return shapes, [pl.BlockSpec((tm, s.shape[1]), tile) for s in shapes]

    shapes_p, specs_p = outputs(xp.shape[0], False, prompt_tile)
    shapes_s, specs_s = outputs(xs.shape[0], True, sample_tile)
    outs = pl.pallas_call(
        functools.partial(_proj_kernel, prompt_tiles=nbp),
        grid=(nbp + nbs,),
        in_specs=in_specs,
        out_specs=specs_p + specs_s,
        out_shape=shapes_p + shapes_s,
        compiler_params=_cparams(("arbitrary",)),
        name="proj",
    )(xp, xs, mod, g, w, qn, kn, ones_q, ones_k, *rope_tabs)
    return outs[:len(shapes_p)], outs[len(shapes_p):]


def _lane_lo(shape):
    return lax.broadcasted_iota(jnp.int32, shape, 1) < HEAD_DIM


def _qk(q, k):
    return lax.dot_general(q, k, (((1,), (1,)), ((), ())), preferred_element_type=F32)


def _values_with_ones(v):
    lo = _lane_lo(v.shape)
    ones = jnp.ones_like(v)
    return jnp.where(lo, v, ones), jnp.where(lo, ones, v)


def _pair_attention_pipeline(work):
    heads = [(item, half) for item in work for half in range(2)]

    def scores_of(item, half):
        load_q, keys, _, bias_fn, _ = item
        q2 = load_q()
        lo = _lane_lo(q2.shape)
        qm = jnp.where(lo if half == 0 else jnp.logical_not(lo), q2, jnp.zeros_like(q2))
        scores = [_qk(qm, k) for k in keys]
        if bias_fn is not None:
            scores[0] = scores[0] + bias_fn(half)
        return scores

    def softmax_of(scores):
        m = functools.reduce(jnp.maximum, [jnp.max(s, axis=-1, keepdims=True) for s in scores])
        return [jnp.exp2(s - m).astype(BF16) for s in scores]

    def output_of(item, half, ps):
        return functools.reduce(jnp.add, [jnp.dot(p, v, preferred_element_type=F32)
                                          for p, v in zip(ps, item[2][half])])

    def finish(item, raw0, raw1):
        lo = _lane_lo(raw0.shape)
        denom = pltpu.roll(jnp.where(lo, raw1, raw0), HEAD_DIM, axis=1)
        item[4](jnp.where(lo, raw0, raw1) * (1.0 / denom))

    n = len(heads)
    scores, probs, raw = {}, {}, {}
    for t in range(n + 4):
        if t < n:
            scores[t] = scores_of(*heads[t])
        if 0 <= t - 1 < n:
            probs[t - 1] = softmax_of(scores.pop(t - 1))
        if 0 <= t - 2 < n:
            item, half = heads[t - 2]
            raw[t - 2] = output_of(item, half, probs.pop(t - 2))
        i = t - 4
        if 0 <= i < n and i % 2 == 0:
            finish(heads[i][0], raw.pop(i), raw.pop(i + 1))


def _pair_attention_small(q2, k2, v2):
    lo = _lane_lo(q2.shape)
    outs = []
    for half in range(2):
        qm = jnp.where(lo if half == 0 else jnp.logical_not(lo), q2, jnp.zeros_like(q2))
        s = _qk(qm, k2)
        p = jnp.exp2(s - jnp.max(s, axis=-1, keepdims=True))
        l = jnp.sum(p, axis=-1, keepdims=True)
        outs.append(jnp.dot(p.astype(BF16), v2, preferred_element_type=F32) * (1.0 / l))
    return jnp.where(lo, outs[0], outs[1])


def _ctx_attn_kernel(naq_ref, nak_ref, nav_ref, gq_ref, gk_ref, gv_ref, ona_ref, ogq_ref):
    gk = _dup_halves(gk_ref[...]).astype(BF16)
    gv = _dup_halves(gv_ref[...]).astype(BF16)
    for j in range(HEAD_PAIRS):
        sl = slice(j * LANES, (j + 1) * LANES)
        kvs = slice((j // 2) * LANES, (j // 2 + 1) * LANES)
        ona_ref[:, sl] = _pair_attention_small(
            naq_ref[:, sl], nak_ref[:, sl].astype(BF16), nav_ref[:, sl].astype(BF16)).astype(BF16)
        ogq_ref[:, sl] = _pair_attention_small(gq_ref[:, sl], gk[:, kvs], gv[:, kvs]).astype(BF16)


def _ctx_attn(naq, nak, nav, gq, gk, gv):
    T = naq.shape[0]
    row = lambda n: pl.BlockSpec((SEQ, n), lambda b: (b, 0))
    return pl.pallas_call(
        _ctx_attn_kernel,
        grid=(T // SEQ,),
        in_specs=[row(NA_W), row(NA_W), row(NA_W), row(GQA_QW), row(GQA_KVW), row(GQA_KVW)],
        out_specs=[row(NA_W), row(GQA_QW)],
        out_shape=[jax.ShapeDtypeStruct((T, NA_W), BF16), jax.ShapeDtypeStruct((T, GQA_QW), BF16)],
        compiler_params=_cparams(("arbitrary",)),
        name="ctx_attn",
    )(naq, nak, nav, gq, gk, gv)


def _na_row_start(r):
    return min(max(r - NA_WIN_R // 2, 0), GRID_ROWS - NA_WIN_R)


def _na_bias_index(rq, rk):
    r0 = _na_row_start(rq)
    if r0 <= rk < r0 + NA_WIN_R:
        return rk - rq + NA_WIN_R - 1
    return None


def _na_latent_kernel(q_ref, k_ref, v_ref, ck_ref, cv_ref, bias_ref, o_ref):
    ck = ck_ref[0, 0].astype(BF16)
    cv = _values_with_ones(cv_ref[0, 0].astype(BF16))
    v = _values_with_ones(v_ref[...])
    qrows = NA_QROWS * GRID_W
    lo = _lane_lo((GRID_W, LANES))
    masked = jnp.full((GRID_W, LANES), NEG_INF, F32)
    work = []
    for qb in range(GRID_ROWS // NA_QROWS):
        w0 = min(max(NA_QROWS * qb - NA_WIN_R // 2, 0), GRID_ROWS - NA_WROWS)
        w0 -= w0 % 2
        win = slice(w0 * GRID_W, (w0 + NA_WROWS) * GRID_W)
        kwin = k_ref[win, :]
        values = tuple([v[half][win], cv[half]] for half in range(2))

        def bias_fn(half, qb=qb, w0=w0):
            rows = []
            for qi in range(NA_QROWS):
                rq = NA_QROWS * qb + qi
                blocks = []
                for kp in range(NA_WROWS // 2):
                    ia = _na_bias_index(rq, w0 + 2 * kp)
                    ib = _na_bias_index(rq, w0 + 2 * kp + 1)
                    blk_a = masked if ia is None else bias_ref[half, ia]
                    blk_b = masked if ib is None else bias_ref[half, ib]
                    blocks.append(masked if ia is None and ib is None else jnp.where(lo, blk_a, blk_b))
                rows.append(jnp.concatenate(blocks, axis=1))
            return jnp.concatenate(rows, axis=0)

        def store(o, qb=qb):
            o_ref[qb * qrows:(qb + 1) * qrows, :] = o.astype(BF16)

        work.append((lambda qb=qb: q_ref[qb * qrows:(qb + 1) * qrows, :], [kwin, ck], values, bias_fn, store))
    _pair_attention_pipeline(work)


def _na_latent(q, k, v, cache_k, cache_v, bias_tab, layer):
    tok = pl.BlockSpec((DEC_SEQ, LANES), lambda b, j: (b, j))
    cache = pl.BlockSpec((1, 1, PAST_LEN, LANES), lambda b, j: (b, layer, 0, j))
    return pl.pallas_call(
        _na_latent_kernel,
        grid=(DEC_BATCH, HEAD_PAIRS),
        in_specs=[tok, tok, tok, cache, cache,
                  pl.BlockSpec((None, 2, 2 * NA_WIN_R - 1, GRID_W, LANES), lambda b, j: (layer, j, 0, 0, 0))],
        out_specs=tok,
        out_shape=jax.ShapeDtypeStruct(q.shape, BF16),
        compiler_params=_cparams(("arbitrary", "arbitrary")),
        name="na_latent",
    )(q, k, v, cache_k, cache_v, bias_tab)


def _gqa_latent_kernel(q_ref, k_ref, v_ref, ck_ref, cv_ref, o_ref):
    group = pl.program_id(1) // (HEAD_PAIRS // GQA_KV_HEADS)

    def both_halves(x):
        lane = lax.broadcasted_iota(jnp.int32, x.shape, 1)
        keep = jnp.where(lane < HEAD_DIM, 0, 1) == group
        return jnp.where(keep, x, pltpu.roll(x, HEAD_DIM, axis=1)).astype(BF16)

    ck = both_halves(ck_ref[0, 0])
    cv = _values_with_ones(both_halves(cv_ref[0, 0]))
    k = k_ref[...]
    v = _values_with_ones(v_ref[...])
    values = tuple([v[half], cv[half]] for half in range(2))
    work = []
    for qb in range(DEC_SEQ // GQA_QBLK):
        sl = slice(qb * GQA_QBLK, (qb + 1) * GQA_QBLK)

        def store(o, sl=sl):
            o_ref[sl, :] = o.astype(BF16)

        work.append((lambda sl=sl: q_ref[sl, :], [k, ck], values, None, store))
    _pair_attention_pipeline(work)


def _gqa_latent(q, k, v, cache_k, cache_v, layer):
    tok = pl.BlockSpec((DEC_SEQ, LANES), lambda b, j: (b, j))
    kv = pl.BlockSpec((DEC_SEQ, LANES), lambda b, j: (b, j // (HEAD_PAIRS // GQA_KV_HEADS)))
    cache = pl.BlockSpec((1, 1, PAST_LEN, GQA_KVW), lambda b, j: (b, layer, 0, 0))
    return pl.pallas_call(
        _gqa_latent_kernel,
        grid=(DEC_BATCH, HEAD_PAIRS),
        in_specs=[tok, kv, kv, cache, cache],
        out_specs=tok,
        out_shape=jax.ShapeDtypeStruct(q.shape, BF16),
        compiler_params=_cparams(("arbitrary", "arbitrary")),
        name="gqa_latent",
    )(q, k, v, cache_k, cache_v)


def _conv_kernel(u_ref, dw_ref, g_ref, b_ref, o_ref, pad_ref, *, L):
    zeros = jnp.zeros((CONV_PAD, CONV_C), F32)
    pad_ref[0:CONV_PAD, :] = zeros
    pad_ref[CONV_PAD + L:2 * CONV_PAD + L, :] = zeros
    pad_ref[CONV_PAD:CONV_PAD + L, :] = u_ref[...]
    first = CONV_PAD - CONV_W // 2

    def chunk(c, carry):
        base = pl.multiple_of(c * CONV_ROWS, CONV_ROWS)
        cols = []
        for lc in range(CONV_C // LANES):
            ls = slice(lc * LANES, (lc + 1) * LANES)
            acc = None
            for r in range(SUBLANES):
                z = None
                for k in range(CONV_W):
                    if (first + k) % SUBLANES != r:
                        continue
                    aligned = first + k - r
                    term = dw_ref[k:k + 1, ls] * pad_ref[pl.ds(base + aligned, CONV_ROWS + SUBLANES), ls]
                    z = term if z is None else z + term
                z = z[r:r + CONV_ROWS]
                acc = z if acc is None else acc + z
            cols.append(acc)
        acc = jnp.concatenate(cols, axis=1)
        mu = jnp.mean(acc, axis=-1, keepdims=True)
        d = acc - mu
        var = jnp.mean(d * d, axis=-1, keepdims=True)
        y = d * lax.rsqrt(var + EPS) * g_ref[...] + b_ref[...]
        o_ref[pl.ds(base, CONV_ROWS), :] = (y * jax.nn.sigmoid(y)).astype(BF16)
        return carry

    lax.fori_loop(0, L // CONV_ROWS, chunk, 0)


def _conv_branch(u, dw, ln_g, ln_b, *, L):
    T = u.shape[0]
    return pl.pallas_call(
        functools.partial(_conv_kernel, L=L),
        grid=(T // L,),
        in_specs=[pl.BlockSpec((L, CONV_C), lambda b: (b, 0)), _resident((CONV_W, CONV_C)),
                  _resident((1, CONV_C)), _resident((1, CONV_C))],
        out_specs=pl.BlockSpec((L, CONV_C), lambda b: (b, 0)),
        out_shape=jax.ShapeDtypeStruct((T, CONV_C), BF16),
        scratch_shapes=[pltpu.VMEM((L + 2 * CONV_PAD, CONV_C), F32)],
        compiler_params=_cparams(("arbitrary",)),
        name="conv_branch",
    )(u, dw, ln_g, ln_b)


def _sigmoid_tanh(x):
    return 0.5 * jnp.tanh(0.5 * x) + 0.5


def _lru_kernel(x_ref, cw_ref, cb_ref, wg_ref, bg_ref, lam_ref, h0_ref, y_ref, st_ref,
                pad_ref, a_ref, b_ref, *, L):
    zeros = jnp.zeros((LRU_PAD, LRU_W), F32)
    first = LRU_PAD - LRU_CONV_W // 2
    decay = LRU_C * jax.nn.log_sigmoid(lam_ref[...])
    for s in range(LRU_NB):
        pad_ref[s, 0:LRU_PAD, :] = zeros
        pad_ref[s, LRU_PAD + L:2 * LRU_PAD + L, :] = zeros
        pad_ref[s, LRU_PAD:LRU_PAD + L, :] = x_ref[s * L:(s + 1) * L, :]
        for c in range(L // LRU_GATE_ROWS):
            r0 = c * LRU_GATE_ROWS
            rows = slice(s * L + r0, s * L + r0 + LRU_GATE_ROWS)
            xc = cb_ref[...] + functools.reduce(jnp.add, [
                cw_ref[k:k + 1, :] * pad_ref[s, r0 + first + k:r0 + first + k + LRU_GATE_ROWS, :]
                for k in range(LRU_CONV_W)])
            xcb = xc.astype(BF16)

            def gate(d, which):
                z = jnp.concatenate([
                    jnp.dot(xcb[:, :LRU_HALF], wg_ref[d, which, 0], preferred_element_type=F32),
                    jnp.dot(xcb[:, LRU_HALF:], wg_ref[d, which, 1], preferred_element_type=F32)], axis=1)
                return _sigmoid_tanh(z + bg_ref[d, which:which + 1, :])

            for d in range(2):
                log_a = decay[d:d + 1, :] * gate(d, 0)
                a = jnp.exp(log_a)
                one_minus_a2 = -jnp.tanh(log_a) * (a * a + 1.0)
                a_ref[d, rows, :] = a
                b_ref[d, rows, :] = jnp.sqrt(one_minus_a2) * (gate(d, 1) * xc)

    def block(i, carry):
        carry = list(carry)
        fwd = pl.multiple_of(i * SUBLANES, SUBLANES)
        bwd = pl.multiple_of(L - SUBLANES - i * SUBLANES, SUBLANES)
        for j in range(SUBLANES):
            for s in range(LRU_NB):
                hf, hb = carry[2 * s], carry[2 * s + 1]
                tf = pl.ds(s * L + fwd + j, 1)
                tb = pl.ds(s * L + bwd + (SUBLANES - 1 - j), 1)
                hf = a_ref[0, tf, :] * hf + b_ref[0, tf, :]
                hb = a_ref[1, tb, :] * hb + b_ref[1, tb, :]
                b_ref[0, tf, :] = hf
                b_ref[1, tb, :] = hb
                carry[2 * s], carry[2 * s + 1] = hf, hb
        return tuple(carry)

    init = []
    for s in range(LRU_NB):
        h0 = h0_ref[s]
        init += [h0[0:1, :], h0[1:2, :]]
    final = lax.fori_loop(0, L // SUBLANES, block, tuple(init))
    y_ref[...] = (b_ref[0] + b_ref[1]).astype(BF16)
    for s in range(LRU_NB):
        st_ref[s] = jnp.concatenate([final[2 * s], final[2 * s + 1]], axis=0)


def _lru_branch(x, cw, cb, wg, bg, lam, h0, *, L):
    T = x.shape[0]
    nb = T // L
    rows = LRU_NB * L
    return pl.pallas_call(
        functools.partial(_lru_kernel, L=L),
        grid=(nb // LRU_NB,),
        in_specs=[pl.BlockSpec((rows, LRU_W), lambda b: (b, 0)), _resident((LRU_CONV_W, LRU_W)),
                  _resident((1, LRU_W)), _resident((2, 2, 2, LRU_HALF, LRU_HALF)),
                  _resident((2, 2, LRU_W)), _resident((2, LRU_W)),
                  pl.BlockSpec((LRU_NB, 2, LRU_W), lambda b: (b, 0, 0))],
        out_specs=[pl.BlockSpec((rows, LRU_W), lambda b: (b, 0)),
                   pl.BlockSpec((LRU_NB, 2, LRU_W), lambda b: (b, 0, 0))],
        out_shape=[jax.ShapeDtypeStruct((T, LRU_W), BF16), jax.ShapeDtypeStruct((nb, 2, LRU_W), F32)],
        scratch_shapes=[pltpu.VMEM((LRU_NB, L + 2 * LRU_PAD, LRU_W), F32), pltpu.VMEM((2, rows, LRU_W), F32),
                        pltpu.VMEM((2, rows, LRU_W), F32)],
        compiler_params=_cparams(("arbitrary",)),
        name="lru_branch",
    )(x, cw, cb, wg, bg, lam, h0)


def _merge_tile(x_ref, mod_ref, g_ref, wga_ref, wgb_ref, b0_ref, b1_ref, b2_ref, b3_ref, wb_ref, wo_ref, o_ref):
    x = x_ref[...]
    mod = mod_ref[0]
    h = _modulate(x, g_ref[2:3], mod, 1).astype(BF16)

    def gate_logits(lo, hi):
        parts = []
        if lo < GATE_A:
            parts.append(jnp.dot(h, wga_ref[:, lo:min(hi, GATE_A)].astype(BF16), preferred_element_type=F32))
        if hi > GATE_A:
            parts.append(jnp.dot(h, wgb_ref[:, max(lo, GATE_A) - GATE_A:hi - GATE_A].astype(BF16),
                                 preferred_element_type=F32))
        return parts[0] if len(parts) == 1 else jnp.concatenate(parts, axis=1)

    merged = None
    for k, br in enumerate((b0_ref, b1_ref, b2_ref, b3_ref)):
        logits = gate_logits(k * D_MODEL, (k + 1) * D_MODEL)
        term = jax.nn.sigmoid(logits) * jnp.dot(br[...], wb_ref[k].astype(BF16), preferred_element_type=F32)
        merged = term if merged is None else merged + term
    y = jnp.dot(merged.astype(BF16), wo_ref[...].astype(BF16), preferred_element_type=F32)
    o_ref[...] = x + mod[5:6] * (_rms(y) * g_ref[3:4])


def _merge_kernel(xp_ref, xs_ref, mod_ref, g_ref, wga_ref, wgb_ref, bp0, bp1, bp2, bp3, bs0, bs1, bs2, bs3,
                  wb_ref, wo_ref, op_ref, os_ref, *, prompt_tiles):
    is_prompt = pl.program_id(0) < prompt_tiles

    @pl.when(is_prompt)
    def _():
        _merge_tile(xp_ref, mod_ref, g_ref, wga_ref, wgb_ref, bp0, bp1, bp2, bp3, wb_ref, wo_ref, op_ref)

    @pl.when(jnp.logical_not(is_prompt))
    def _():
        _merge_tile(xs_ref, mod_ref, g_ref, wga_ref, wgb_ref, bs0, bs1, bs2, bs3, wb_ref, wo_ref, os_ref)


def _merge(xp, xs, mod, g, w_in, branches_p, branches_s, w_branch, w_out, *, layer):
    tm = TM_MERGE
    nbp = xp.shape[0] // tm
    nbs = xs.shape[0] // tm
    prompt_tile = lambda i: (jnp.minimum(i, nbp - 1), 0)
    sample_tile = lambda i: (jnp.maximum(i - nbp, 0), 0)
    return pl.pallas_call(
        functools.partial(_merge_kernel, prompt_tiles=nbp),
        grid=(nbp + nbs,),
        in_specs=[pl.BlockSpec((tm, D_MODEL), prompt_tile), pl.BlockSpec((tm, D_MODEL), sample_tile),
                  pl.BlockSpec((1, 9, D_MODEL), lambda i: (i * tm // DEC_SEQ, 0, 0)),
                  _resident((6, D_MODEL)),
                  pl.BlockSpec((None, D_MODEL, GATE_A), lambda i: (layer, 0, OFF_GATE // GATE_A),
                               pipeline_mode=pl.Buffered(1)),
                  pl.BlockSpec((None, D_MODEL, GATE_B), lambda i: (layer, 0, (OFF_GATE + GATE_A) // GATE_B),
                               pipeline_mode=pl.Buffered(1))]
        + [pl.BlockSpec((tm, BRANCH_W), prompt_tile)] * N_BRANCH
        + [pl.BlockSpec((tm, BRANCH_W), sample_tile)] * N_BRANCH
        + [_stacked((layer,), (N_BRANCH, BRANCH_W, D_MODEL)), _stacked((layer,), (D_MODEL, D_MODEL))],
        out_specs=[pl.BlockSpec((tm, D_MODEL), prompt_tile), pl.BlockSpec((tm, D_MODEL), sample_tile)],
        out_shape=[jax.ShapeDtypeStruct(xp.shape, F32), jax.ShapeDtypeStruct(xs.shape, F32)],
        compiler_params=_cparams(("arbitrary",)),
        name="merge",
    )(xp, xs, mod, g, w_in, w_in, *branches_p, *branches_s, w_branch, w_out)


def _rope_tables():
    t = np.arange(DEC_SEQ)
    row = (t // GRID_W).astype(np.float32)
    col = (t % GRID_W).astype(np.float32)
    half = HEAD_DIM // 2
    freqs = (np.float32(ROPE_THETA) ** (-np.arange(0, half, 2, dtype=np.float32) / np.float32(half))).astype(np.float32)
    ar = row[:, None] * freqs
    ac = col[:, None] * freqs
    cos_h = np.concatenate([np.cos(ar), np.cos(ar), np.cos(ac), np.cos(ac)], axis=1)
    sin_h = np.concatenate([-np.sin(ar), np.sin(ar), -np.sin(ac), np.sin(ac)], axis=1)
    return (jnp.asarray(np.tile(cos_h, (1, 2)), F32), jnp.asarray(np.tile(sin_h, (1, 2)), F32))


def _na_bias_tables(na_rpb):
    c = np.arange(GRID_W)
    c0 = np.clip(c - NA_WIN_C // 2, 0, GRID_W - NA_WIN_C)
    col_ok = (c[None, :] >= c0[:, None]) & (c[None, :] < c0[:, None] + NA_WIN_C)
    dc = np.clip(c[None, :] - c[:, None], -(NA_WIN_C - 1), NA_WIN_C - 1) + NA_WIN_C - 1
    col_ok = np.concatenate([col_ok, col_ok], axis=1)
    dc = np.concatenate([dc, dc], axis=1)
    onehot = (np.arange(2 * NA_WIN_C - 1)[:, None, None] == dc[None]) & col_ok[None]
    tab = jnp.einsum('lhrm,mqk->lhrqk', na_rpb.astype(F32) * LOG2E, jnp.asarray(onehot, F32),
                     precision=lax.Precision.HIGHEST)
    return tab + jnp.asarray(np.where(col_ok, 0.0, NEG_INF), F32)


def _block_diag_ones(n):
    return jnp.asarray(np.kron(np.eye(n // HEAD_DIM), np.ones((HEAD_DIM, HEAD_DIM))), BF16)


def _lru_gate_weights(lru_wr, lru_wi):
    w = jnp.stack([lru_wr, lru_wi], axis=2)
    per_half = LRU_BLOCKS // 2
    bw = LRU_W // LRU_BLOCKS
    w = w.reshape(DEPTH, 2, 2, 2, per_half, bw, bw)
    eye = jnp.asarray(np.eye(per_half), F32)
    bd = jnp.einsum('...nij,nm->...nimj', w, eye).reshape(DEPTH, 2, 2, 2, LRU_HALF, LRU_HALF)
    return bd.astype(BF16)


def kernel(x_prompt, x_sample, c, cache_na_k, cache_na_v, cache_gqa_k, cache_gqa_v, state_lru, c_ctx, w_ada, b_ada, norm_g, ffn_w1, ffn_w2, w_in, na_rpb, conv_dw, conv_ln_g, conv_ln_b, gqa_q_norm, gqa_k_norm, lru_conv_w, lru_conv_b, lru_wr, lru_br, lru_wi, lru_bi, lru_lambda, w_branch, w_out):
    tp = BATCH * SEQ
    ts = DEC_BATCH * DEC_SEQ
    c_all = jnp.concatenate([c_ctx[None, :], c, jnp.zeros((ADA_ROWS - 1 - DEC_BATCH, D_MODEL), F32)], axis=0)
    mods = _adaln(c_all, w_ada, b_ada).reshape(DEPTH, ADA_ROWS, 9, D_MODEL)

    w1, w2, w_in16, wb, wo = ffn_w1, ffn_w2, w_in, w_branch, w_out
    lru_wg = _lru_gate_weights(lru_wr, lru_wi)
    lru_bg = jnp.stack([lru_br, lru_bi], axis=2)
    bias_tab = _na_bias_tables(na_rpb)
    rope_tabs = _rope_tables()
    ones_q = _block_diag_ones(GQA_QW)
    ones_k = _block_diag_ones(GQA_KVW)
    cna_k = cache_na_k.reshape(DEC_BATCH, DEPTH, PAST_LEN, NA_W)
    cna_v = cache_na_v.reshape(DEC_BATCH, DEPTH, PAST_LEN, NA_W)
    cgq_k = cache_gqa_k.reshape(DEC_BATCH, DEPTH, PAST_LEN, GQA_KVW)
    cgq_v = cache_gqa_v.reshape(DEC_BATCH, DEPTH, PAST_LEN, GQA_KVW)
    h0_prompt = jnp.zeros((BATCH, 2, LRU_W), F32)

    xp = x_prompt.reshape(tp, D_MODEL)
    xs = x_sample.reshape(ts, D_MODEL)
    nk_l, nv_l, gk_l, gv_l, st_l = [], [], [], [], []
    for l in range(DEPTH):
        g = norm_g[l]
        mod_p = mods[l, 0:1]
        mod_s = mods[l, 1:1 + DEC_BATCH]
        qn = jnp.tile(gqa_q_norm[l], GQA_HEADS)[None, :]
        kn = jnp.tile(gqa_k_norm[l], GQA_KV_HEADS)[None, :]
        lru_args = (lru_conv_w[l], lru_conv_b[l][None, :], lru_wg[l], lru_bg[l], lru_lambda[l])
        conv_args = (conv_dw[l], conv_ln_g[l][None, :], conv_ln_b[l][None, :])

        mod_all = jnp.concatenate([jnp.broadcast_to(mod_p, (tp // DEC_SEQ, 9, D_MODEL)), mod_s], axis=0)

        xp, xs = _ffn(xp, xs, mod_all, g, w1, w2, layer=l, slot=0)

        proj_p, proj_s = _proj(xp, xs, mod_all, g, w_in16, qn, kn, ones_q, ones_k, rope_tabs, layer=l)

        naq, nak, nav, u, gq, gk, gv, lx = proj_p
        o_na, o_gq = _ctx_attn(naq, nak, nav, gq, gk, gv)
        o_conv = _conv_branch(u, *conv_args, L=SEQ)
        o_lru, st = _lru_branch(lx, *lru_args, h0_prompt, L=SEQ)
        branches_p = (o_na, o_conv, o_gq, o_lru)
        nk_l.append(nak.reshape(BATCH, SEQ, NA_HEADS, HEAD_DIM))
        nv_l.append(nav.reshape(BATCH, SEQ, NA_HEADS, HEAD_DIM))
        gk_l.append(gk.reshape(BATCH, SEQ, GQA_KV_HEADS, HEAD_DIM))
        gv_l.append(gv.reshape(BATCH, SEQ, GQA_KV_HEADS, HEAD_DIM))
        st_l.append(st)

        naq, nak, nav, u, gq, gk, gv, lx = proj_s
        o_na = _na_latent(naq, nak, nav, cna_k, cna_v, bias_tab, l)
        o_gq = _gqa_latent(gq, gk, gv, cgq_k, cgq_v, l)
        o_conv = _conv_branch(u, *conv_args, L=DEC_SEQ)
        o_lru, _ = _lru_branch(lx, *lru_args, state_lru[:, l], L=DEC_SEQ)
        branches_s = (o_na, o_conv, o_gq, o_lru)

        xp, xs = _merge(xp, xs, mod_all, g, w_in16, branches_p, branches_s, wb, wo, layer=l)
        xp, xs = _ffn(xp, xs, mod_all, g, w1, w2, layer=l, slot=1)

    return (xp.reshape(BATCH, SEQ, D_MODEL), xs.reshape(DEC_BATCH, DEC_SEQ, D_MODEL),
            jnp.stack(nk_l, axis=1), jnp.stack(nv_l, axis=1), jnp.stack(gk_l, axis=1), jnp.stack(gv_l, axis=1),
            jnp.stack(st_l, axis=1))
```
